```python
import jax, jax.numpy as jnp
from jax import lax
import numpy as np

D_MODEL = 1024
BATCH = 8
SEQ = 4096
DEPTH = 1
DEC_BATCH = 8
DEC_SEQ = 8192
PAST_LEN = 128

GRID_W = 64
D_CONV = 1024
CONV_K = 31
N_HEADS = 16
HEAD_DIM = 64
D_ATTN = N_HEADS * HEAD_DIM
WIN_ROWS_MAX = 8
WIN_COLS = 16
Q_COL_BLOCK = 16
K_COL_BAND = Q_COL_BLOCK + WIN_COLS
N_COL_BLOCKS = GRID_W // Q_COL_BLOCK
D_IN = 3 * D_CONV + 4 * D_ATTN + 2 * D_MODEL
RMS_EPS = 1e-6
LN_EPS = 1e-5
NEG_INF = -1e30

kernel_name = "gated_conformer_conv_neighbourhood_attn_encoder"


def _rmsnorm(x, g):
    xf = x.astype(jnp.float32)
    y = xf * lax.rsqrt(jnp.mean(xf * xf, axis=-1, keepdims=True) + RMS_EPS)
    return (y * g.astype(jnp.float32)).astype(x.dtype)


def _layernorm(x, g, b):
    xf = x.astype(jnp.float32)
    mu = jnp.mean(xf, axis=-1, keepdims=True)
    var = jnp.mean(jnp.square(xf - mu), axis=-1, keepdims=True)
    y = (xf - mu) * lax.rsqrt(var + LN_EPS)
    return (y * g.astype(jnp.float32) + b.astype(jnp.float32)).astype(x.dtype)


def _col_tables():
    j = np.arange(N_COL_BLOCKS)
    band_start = np.clip(j * Q_COL_BLOCK - WIN_COLS // 2, 0, GRID_W - K_COL_BAND)
    key_col = band_start[:, None] + np.arange(K_COL_BAND)[None, :]
    q_col = j[:, None] * Q_COL_BLOCK + np.arange(Q_COL_BLOCK)[None, :]
    win_start = np.clip(q_col - WIN_COLS // 2, 0, GRID_W - WIN_COLS)
    kc = key_col[:, None, :]
    valid = (kc >= win_start[..., None]) & (kc < win_start[..., None] + WIN_COLS)
    col_idx = np.clip(kc - q_col[..., None], -(WIN_COLS - 1), WIN_COLS - 1) + WIN_COLS - 1
    return key_col, valid, col_idx


def _neighbourhood_attention(q, k, v, rpb):
    bsz, seq_len = q.shape[0], q.shape[1]
    rows = seq_len // GRID_W
    kh = min(WIN_ROWS_MAX, rows)
    grid = (bsz, rows, GRID_W, N_HEADS, HEAD_DIM)
    q, k, v = q.reshape(grid), k.reshape(grid), v.reshape(grid)
    key_col, valid, col_idx = _col_tables()
    valid_m = jnp.asarray(valid)[:, :, None, None, :]
    scale = HEAD_DIM ** -0.5

    def row_step(args):
        r, q_row = args
        rs = jnp.clip(r - kh // 2, 0, rows - kh)
        k_band = lax.dynamic_slice_in_dim(k, rs, kh, axis=1)[:, :, key_col]
        v_band = lax.dynamic_slice_in_dim(v, rs, kh, axis=1)[:, :, key_col]
        qb = q_row.reshape(bsz, N_COL_BLOCKS, Q_COL_BLOCK, N_HEADS, HEAD_DIM)
        s = jnp.einsum('bjqhd,bijkhd->bjqhik', qb, k_band,
                       preferred_element_type=jnp.float32) * scale
        row_idx = rs + jnp.arange(kh) - r + WIN_ROWS_MAX - 1
        rpb_r = rpb[:, row_idx, :].astype(jnp.float32)
        bias = rpb_r[:, :, col_idx].transpose(2, 3, 0, 1, 4)
        s = jnp.where(valid_m, s + bias, NEG_INF)
        p = jax.nn.softmax(s.reshape(s.shape[:4] + (kh * K_COL_BAND,)), axis=-1).reshape(s.shape)
        o = jnp.einsum('bjqhik,bijkhd->bjqhd', p.astype(v_band.dtype), v_band)
        return o.reshape(bsz, GRID_W, N_HEADS, HEAD_DIM)

    out = lax.map(row_step, (jnp.arange(rows), q.transpose(1, 0, 2, 3, 4)))
    return out.transpose(1, 0, 2, 3, 4).reshape(bsz, seq_len, D_ATTN)


def _conv_module(a_val, a_glu, dw_w, dw_b, ln_g, ln_b):
    u = a_val * jax.nn.sigmoid(a_glu)
    c = lax.conv_general_dilated(
        u, dw_w[:, None, :].astype(u.dtype), window_strides=(1,),
        padding=[(CONV_K // 2, CONV_K // 2)],
        dimension_numbers=('NWC', 'WIO', 'NWC'),
        feature_group_count=D_CONV) + dw_b
    return jax.nn.silu(_layernorm(c, ln_g, ln_b))


def _layer(x, norm_g, w_in, b_gate, dw_w, dw_b, ln_g, ln_b, w_pw_a, rpb, w_o_b, w_out):
    bsz, seq_len, _ = x.shape
    h = _rmsnorm(x, norm_g)
    proj = h @ w_in
    sizes = [D_CONV, D_CONV, D_CONV, D_ATTN, D_ATTN, D_ATTN, D_ATTN, 2 * D_MODEL]
    a_val, a_glu, z_a, q, k, v, z_b, g_logits = jnp.split(proj, np.cumsum(sizes)[:-1].tolist(), axis=-1)
    y_a = (_conv_module(a_val, a_glu, dw_w, dw_b, ln_g, ln_b) * jax.nn.silu(z_a)) @ w_pw_a
    hs = (bsz, seq_len, N_HEADS, HEAD_DIM)
    attn = _neighbourhood_attention(q.reshape(hs), k.reshape(hs), v.reshape(hs), rpb)
    y_b = (attn * jax.nn.silu(z_b)) @ w_o_b
    gates = jax.nn.sigmoid(g_logits + b_gate)
    g_a, g_b = gates[..., :D_MODEL], gates[..., D_MODEL:]
    merged = g_a * y_a + g_b * y_b
    return x + merged @ w_out


def _trunk(x, norm_g, w_in, b_gate, dw_w, dw_b, ln_g, ln_b, w_pw_a, rpb, w_o_b, w_out, final_g):
    for l in range(DEPTH):
        x = _layer(x, norm_g[l], w_in[l], b_gate[l], dw_w[l], dw_b[l], ln_g[l], ln_b[l],
                   w_pw_a[l], rpb[l], w_o_b[l], w_out[l])
    return _rmsnorm(x, final_g)


def setup_inputs(seed: int = 0) -> dict:
    key = jax.random.key(seed)
    ks = jax.random.split(key, 16)
    f32 = jnp.float32
    nrm = lambda k, shape, s: jax.random.normal(k, shape, f32) * s
    return {
        "x_prompt": nrm(ks[0], (BATCH, SEQ, D_MODEL), 1.0),
        "x_sample": nrm(ks[1], (DEC_BATCH, DEC_SEQ, D_MODEL), 1.0),
        "norm_g": 1.0 + nrm(ks[2], (DEPTH, D_MODEL), 0.02),
        "w_in": nrm(ks[3], (DEPTH, D_MODEL, D_IN), D_MODEL ** -0.5),
        "b_gate": nrm(ks[4], (DEPTH, 2 * D_MODEL), 0.01),
        "dw_w": nrm(ks[5], (DEPTH, CONV_K, D_CONV), CONV_K ** -0.5),
        "dw_b": nrm(ks[6], (DEPTH, D_CONV), 0.02),
        "ln_g": 1.0 + nrm(ks[7], (DEPTH, D_CONV), 0.02),
        "ln_b": nrm(ks[8], (DEPTH, D_CONV), 0.02),
        "w_pw_a": nrm(ks[9], (DEPTH, D_CONV, D_MODEL), D_CONV ** -0.5),
        "rpb": nrm(ks[10], (DEPTH, N_HEADS, 2 * WIN_ROWS_MAX - 1, 2 * WIN_COLS - 1), 0.1),
        "w_o_b": nrm(ks[11], (DEPTH, D_ATTN, D_MODEL), D_ATTN ** -0.5),
        "w_out": nrm(ks[12], (DEPTH, D_MODEL, D_MODEL), D_MODEL ** -0.5),
        "final_g": 1.0 + nrm(ks[13], (D_MODEL,), 0.02),
    }


def reference(x_prompt, x_sample, norm_g, w_in, b_gate, dw_w, dw_b, ln_g, ln_b, w_pw_a, rpb, w_o_b, w_out, final_g):
    y_prompt = _trunk(x_prompt, norm_g, w_in, b_gate, dw_w, dw_b, ln_g, ln_b, w_pw_a, rpb, w_o_b, w_out, final_g)
    y_sample = _trunk(x_sample, norm_g, w_in, b_gate, dw_w, dw_b, ln_g, ln_b, w_pw_a, rpb, w_o_b, w_out, final_g)
    return (y_prompt, y_sample)
```

```python
import functools
import math

import numpy as np
import jax
import jax.numpy as jnp
from jax import lax
from jax.experimental import pallas as pl
from jax.experimental.pallas import tpu as pltpu

D_MODEL = 1024
GRID_W = 64
CONV_K = 31
CONV_HALO = 16
N_HEADS = 16
HEAD_DIM = 64
WIN_ROWS = 8
WIN_COLS = 16
HEADS_PER_GROUP = 4
GROUP_W = HEADS_PER_GROUP * HEAD_DIM
N_GROUPS = N_HEADS // HEADS_PER_GROUP
N_PROJ_CHUNKS = 9
COL_A_VAL, COL_A_GLU, COL_Z_A, COL_Q, COL_K, COL_V, COL_Z_B, COL_G_A, COL_G_B = range(N_PROJ_CHUNKS)
RMS_EPS = 1e-6
LN_EPS = 1e-5
NEG_INF = -1e30
LOG2E = math.log2(math.e)
Q_SCALE = HEAD_DIM ** -0.5 * LOG2E

VMEM_LIMIT_BYTES = 56 * 1024 * 1024

TM_PROJ = 512
TS_CONV = 512
ROWS_PER_STEP = 8
TM_MERGE = 512

f32 = jnp.float32
bf16 = jnp.bfloat16


def _sigmoid(x):
    return 1.0 / (1.0 + jnp.exp(-x))


def _silu(x):
    return x * _sigmoid(x)


def _const_spec(shape):
    return pl.BlockSpec(shape, lambda *_: (0,) * len(shape), pipeline_mode=pl.Buffered(1))


def _inproj_kernel(x_ref, g_ref, w_ref, o_ref):
    x = x_ref[...]
    ms = jnp.mean(x * x, axis=-1, keepdims=True)
    h = (x * lax.rsqrt(ms + RMS_EPS) * g_ref[...]).astype(bf16)
    for n in range(N_PROJ_CHUNKS):
        cols = slice(n * D_MODEL, (n + 1) * D_MODEL)
        acc = jnp.dot(h, w_ref[:, cols], preferred_element_type=f32)
        if n == COL_Q:
            acc = acc * Q_SCALE
        o_ref[:, cols] = acc.astype(bf16)


def _in_proj(x2d, norm_g, w_in_bf16):
    t = x2d.shape[0]
    d_in = w_in_bf16.shape[1]
    return pl.pallas_call(
        _inproj_kernel,
        grid=(t // TM_PROJ,),
        in_specs=[
            pl.BlockSpec((TM_PROJ, D_MODEL), lambda i: (i, 0)),
            _const_spec((1, D_MODEL)),
            _const_spec((D_MODEL, d_in)),
        ],
        out_specs=pl.BlockSpec((TM_PROJ, d_in), lambda i: (i, 0)),
        out_shape=jax.ShapeDtypeStruct((t, d_in), bf16),
        compiler_params=pltpu.CompilerParams(
            dimension_semantics=("arbitrary",), vmem_limit_bytes=VMEM_LIMIT_BYTES),
        name="in_proj",
    )(x2d, norm_g, w_in_bf16)


CONV_ROW_CHUNK = 64
CONV_LANE_CHUNK = 128
LN_ROW_CHUNK = 16


def _conv_kernel(av_ref, ag_ref, za_ref, avp_ref, agp_ref, avn_ref, agn_ref,
                 dww_ref, dwb_ref, lng_ref, lnb_ref, o_ref, u_ref, c_ref):
    i = pl.program_id(1)
    n_tiles = pl.num_programs(1)
    ts = av_ref.shape[1]

    def glu(a_ref, g_ref):
        return a_ref[0].astype(f32) * _sigmoid(g_ref[0].astype(f32))

    u_ref[0:CONV_HALO, :] = jnp.where(i > 0, glu(avp_ref, agp_ref), 0.0)
    u_ref[CONV_HALO:CONV_HALO + ts, :] = glu(av_ref, ag_ref)
    u_ref[CONV_HALO + ts:, :] = jnp.where(i < n_tiles - 1, glu(avn_ref, agn_ref), 0.0)

    base = CONV_HALO - CONV_K // 2

    def conv_chunk(idx, carry):
        lanes = pl.ds(pl.multiple_of(idx * CONV_LANE_CHUNK, CONV_LANE_CHUNK), CONV_LANE_CHUNK)
        w = dww_ref[:, lanes]
        for r0 in range(0, ts, CONV_ROW_CHUNK):
            acc = jnp.broadcast_to(dwb_ref[:, lanes], (CONV_ROW_CHUNK, CONV_LANE_CHUNK))
            for k in range(CONV_K):
                acc = acc + w[k:k + 1, :] * u_ref[r0 + base + k:r0 + base + k + CONV_ROW_CHUNK, lanes]
            c_ref[r0:r0 + CONV_ROW_CHUNK, lanes] = acc
        return carry

    lax.fori_loop(0, D_MODEL // CONV_LANE_CHUNK, conv_chunk, 0)

    def ln_chunk(idx, carry):
        r0 = pl.multiple_of(idx * LN_ROW_CHUNK, LN_ROW_CHUNK)
        rows = pl.ds(r0, LN_ROW_CHUNK)
        c = c_ref[rows, :]
        mu = jnp.mean(c, axis=-1, keepdims=True)
        d = c - mu
        var = jnp.mean(d * d, axis=-1, keepdims=True)
        y = d * lax.rsqrt(var + LN_EPS) * lng_ref[...] + lnb_ref[...]
        o_ref[0, rows, :] = (_silu(y) * _silu(za_ref[0, rows, :].astype(f32))).astype(bf16)
        return carry

    lax.fori_loop(0, ts // LN_ROW_CHUNK, ln_chunk, 0)


def _conv_branch(proj3d, dw_w, dw_b, ln_g, ln_b):
    b, l, _ = proj3d.shape
    ts = TS_CONV
    halo_blocks_per_tile = ts // CONV_HALO
    n_halo_blocks = l // CONV_HALO

    def main(col):
        return pl.BlockSpec((1, ts, D_MODEL), lambda bi, i: (bi, i, col))

    def prev(col):
        return pl.BlockSpec((1, CONV_HALO, D_MODEL),
                            lambda bi, i: (bi, jnp.maximum(i * halo_blocks_per_tile - 1, 0), col))

    def nxt(col):
        return pl.BlockSpec((1, CONV_HALO, D_MODEL),
                            lambda bi, i: (bi, jnp.minimum((i + 1) * halo_blocks_per_tile, n_halo_blocks - 1), col))

    return pl.pallas_call(
        _conv_kernel,
        grid=(b, l // ts),
        in_specs=[
            main(COL_A_VAL), main(COL_A_GLU), main(COL_Z_A),
            prev(COL_A_VAL), prev(COL_A_GLU), nxt(COL_A_VAL), nxt(COL_A_GLU),
            _const_spec((CONV_K, D_MODEL)), _const_spec((1, D_MODEL)),
            _const_spec((1, D_MODEL)), _const_spec((1, D_MODEL)),
        ],
        out_specs=pl.BlockSpec((1, ts, D_MODEL), lambda bi, i: (bi, i, 0)),
        out_shape=jax.ShapeDtypeStruct((b, l, D_MODEL), bf16),
        scratch_shapes=[
            pltpu.VMEM((ts + 2 * CONV_HALO, D_MODEL), f32),
            pltpu.VMEM((ts, D_MODEL), f32),
        ],
        compiler_params=pltpu.CompilerParams(
            dimension_semantics=("arbitrary", "arbitrary"), vmem_limit_bytes=VMEM_LIMIT_BYTES),
        name="conv_branch",
    )(proj3d, proj3d, proj3d, proj3d, proj3d, proj3d, proj3d, dw_w, dw_b, ln_g, ln_b)


def _bias_table(rpb):
    qc = np.arange(GRID_W)[None, :]
    kc = np.arange(GRID_W)[:, None]
    win_start = np.clip(qc - WIN_COLS // 2, 0, GRID_W - WIN_COLS)
    valid = (kc >= win_start) & (kc < win_start + WIN_COLS)
    col_idx = np.clip(kc - qc, -(WIN_COLS - 1), WIN_COLS - 1) + WIN_COLS - 1
    t = rpb.astype(f32)[:, :, col_idx] * LOG2E
    t = jnp.where(jnp.asarray(valid)[None, None], t, NEG_INF)
    n_dr = 2 * WIN_ROWS - 1
    return t.transpose(1, 2, 0, 3).reshape(n_dr * GRID_W, N_HEADS * GRID_W)


def _attn_kernel(q_ref, zb_ref, k_ref, v_ref, t_ref, o_ref, *, n_rows):
    step = pl.program_id(1)
    win = WIN_ROWS * GRID_W
    lane_group = lax.broadcasted_iota(jnp.int32, (GRID_W, GROUP_W), 1) // HEAD_DIM

    def row_body(j, carry):
        r = step * ROWS_PER_STEP + j
        rs = jnp.clip(r - WIN_ROWS // 2, 0, n_rows - WIN_ROWS)
        k0 = pl.multiple_of(rs * GRID_W, GRID_W)
        t0 = pl.multiple_of((rs - r + WIN_ROWS - 1) * GRID_W, GRID_W)
        q0 = pl.multiple_of(j * GRID_W, GRID_W)
        for g in range(N_GROUPS):
            cols = slice(g * GROUP_W, (g + 1) * GROUP_W)
            qg = q_ref[0, pl.ds(q0, GRID_W), cols]
            zero = jnp.zeros_like(qg)
            qbd = jnp.concatenate(
                [jnp.where(lane_group == h, qg, zero) for h in range(HEADS_PER_GROUP)], axis=0)
            kw = k_ref[0, pl.ds(k0, win), cols]
            s = lax.dot_general(kw, qbd, (((1,), (1,)), ((), ())),
                                preferred_element_type=f32)
            s = s + t_ref[pl.ds(t0, win), cols]
            m = jnp.max(s, axis=0, keepdims=True)
            e = jnp.exp2(s - m)
            p = (e * (1.0 / jnp.sum(e, axis=0, keepdims=True))).astype(bf16)
            vw = v_ref[0, pl.ds(k0, win), cols]
            o2 = lax.dot_general(p, vw, (((0,), (0,)), ((), ())),
                                 preferred_element_type=f32)
            og = jnp.zeros((GRID_W, GROUP_W), f32)
            for h in range(HEADS_PER_GROUP):
                og = jnp.where(lane_group == h, o2[h * GRID_W:(h + 1) * GRID_W, :], og)
            zb = zb_ref[0, pl.ds(q0, GRID_W), cols].astype(f32)
            o_ref[0, pl.ds(q0, GRID_W), cols] = (og * _silu(zb)).astype(bf16)
        return carry

    lax.fori_loop(0, ROWS_PER_STEP, row_body, 0)


def _attn_branch(proj3d, bias_table):
    b, l, _ = proj3d.shape
    n_rows = l // GRID_W
    tq = ROWS_PER_STEP * GRID_W

    def tile(col):
        return pl.BlockSpec((1, tq, D_MODEL), lambda bi, i: (bi, i, col))

    def whole(col):
        return pl.BlockSpec((1, l, D_MODEL), lambda bi, i: (bi, 0, col), pipeline_mode=pl.Buffered(1))

    return pl.pallas_call(
        functools.partial(_attn_kernel, n_rows=n_rows),
        grid=(b, n_rows // ROWS_PER_STEP),
        in_specs=[tile(COL_Q), tile(COL_Z_B), whole(COL_K), whole(COL_V),
                  _const_spec(bias_table.shape)],
        out_specs=pl.BlockSpec((1, tq, D_MODEL), lambda bi, i: (bi, i, 0)),
        out_shape=jax.ShapeDtypeStruct((b, l, D_MODEL), bf16),
        compiler_params=pltpu.CompilerParams(
            dimension_semantics=("arbitrary", "arbitrary"), vmem_limit_bytes=VMEM_LIMIT_BYTES),
        name="attn_branch",
    )(proj3d, proj3d, proj3d, proj3d, bias_table)


def _merge_kernel(x_ref, sa_ref, sb_ref, ga_ref, gb_ref, bg_ref, wa_ref, wb_ref, wo_ref, fg_ref, o_ref):
    y_a = jnp.dot(sa_ref[...], wa_ref[...], preferred_element_type=f32)
    y_b = jnp.dot(sb_ref[...], wb_ref[...], preferred_element_type=f32)
    g_a = _sigmoid(ga_ref[...].astype(f32) + bg_ref[:, :D_MODEL])
    g_b = _sigmoid(gb_ref[...].astype(f32) + bg_ref[:, D_MODEL:])
    merged = (g_a * y_a + g_b * y_b).astype(bf16)
    out = x_ref[...] + jnp.dot(merged, wo_ref[...], preferred_element_type=f32)
    ms = jnp.mean(out * out, axis=-1, keepdims=True)
    o_ref[...] = out * lax.rsqrt(ms + RMS_EPS) * fg_ref[...]


def _merge(x2d, s_a, s_b, proj2d, b_gate, w_pw_a, w_o_b, w_out, final_g):
    t = x2d.shape[0]
    tm = TM_MERGE

    def tile(col=0):
        return pl.BlockSpec((tm, D_MODEL), lambda i: (i, col))

    return pl.pallas_call(
        _merge_kernel,
        grid=(t // tm,),
        in_specs=[tile(), tile(), tile(), tile(COL_G_A), tile(COL_G_B),
                  _const_spec((1, 2 * D_MODEL)),
                  _const_spec((D_MODEL, D_MODEL)), _const_spec((D_MODEL, D_MODEL)),
                  _const_spec((D_MODEL, D_MODEL)), _const_spec((1, D_MODEL))],
        out_specs=tile(),
        out_shape=jax.ShapeDtypeStruct((t, D_MODEL), f32),
        compiler_params=pltpu.CompilerParams(
            dimension_semantics=("arbitrary",), vmem_limit_bytes=VMEM_LIMIT_BYTES),
        name="merge_out",
    )(x2d, s_a, s_b, proj2d, proj2d, b_gate, w_pw_a, w_o_b, w_out, final_g)


def _trunk(x, norm_g, w_in, b_gate, dw_w, dw_b, ln_g, ln_b, w_pw_a, rpb, w_o_b, w_out, final_g):
    assert norm_g.shape[0] == 1, "one layer"
    b, l, d = x.shape
    assert d == D_MODEL and l % (ROWS_PER_STEP * GRID_W) == 0 and l % TS_CONV == 0
    assert (b * l) % TM_PROJ == 0 and (b * l) % TM_MERGE == 0
    row = lambda v: v.reshape(1, -1)
    x2d = x.reshape(b * l, d)
    proj2d = _in_proj(x2d, row(norm_g[0]), w_in[0].astype(bf16))
    proj3d = proj2d.reshape(b, l, -1)
    s_a = _conv_branch(proj3d, dw_w[0], row(dw_b[0]), row(ln_g[0]), row(ln_b[0]))
    s_b = _attn_branch(proj3d, _bias_table(rpb[0]))
    y = _merge(x2d, s_a.reshape(b * l, d), s_b.reshape(b * l, d), proj2d, row(b_gate[0]),
               w_pw_a[0].astype(bf16), w_o_b[0].astype(bf16), w_out[0].astype(bf16), row(final_g))
    return y.reshape(b, l, d)


def kernel(x_prompt, x_sample, norm_g, w_in, b_gate, dw_w, dw_b, ln_g, ln_b, w_pw_a, rpb, w_o_b, w_out, final_g):
    params = (norm_g, w_in, b_gate, dw_w, dw_b, ln_g, ln_b, w_pw_a, rpb, w_o_b, w_out, final_g)
    return (_trunk(x_prompt, *params), _trunk(x_sample, *params))
```

```python
import functools
import math

import numpy as np
import jax
import jax.numpy as jnp
from jax import lax
from jax.experimental import pallas as pl
from jax.experimental.pallas import tpu as pltpu

D_MODEL = 1024
GRID_W = 64
CONV_K = 31
CONV_HALO = 16
SUBLANES = 8
LANES = 128
N_HEADS = 16
HEAD_DIM = 64
WIN_ROWS = 8
WIN_COLS = 16
HEADS_PER_GROUP = 4
GROUP_W = HEADS_PER_GROUP * HEAD_DIM
N_GROUPS = N_HEADS // HEADS_PER_GROUP
N_PROJ_CHUNKS = 8
COL_A_VAL, COL_A_GLU, COL_Z_A, COL_Q, COL_K, COL_Z_B, COL_G_A, COL_G_B = range(N_PROJ_CHUNKS)
W_IN_V_CHUNK = 5
RMS_EPS = 1e-6
LN_EPS = 1e-5
NEG_INF = -1e30
LOG2E = math.log2(math.e)
Q_SCALE = HEAD_DIM ** -0.5 * LOG2E

VMEM_LIMIT_BYTES = 56 * 1024 * 1024

TM_PROJ = 512
TS_CONV = 512
ROWS_PER_STEP = 8
TM_MERGE = 512

f32 = jnp.float32
bf16 = jnp.bfloat16


def _sigmoid(x):
    return 1.0 / (1.0 + jnp.exp(-x))


def _silu(x):
    return x * _sigmoid(x)


def _const_spec(shape):
    return pl.BlockSpec(shape, lambda *_: (0,) * len(shape), pipeline_mode=pl.Buffered(1))


def _inproj_kernel(x_ref, g_ref, w_ref, wvt_ref, o_ref, vt_ref):
    x = x_ref[...]
    ms = jnp.mean(x * x, axis=-1, keepdims=True)
    h = (x * lax.rsqrt(ms + RMS_EPS) * g_ref[...]).astype(bf16)
    for n in range(N_PROJ_CHUNKS):
        cols = slice(n * D_MODEL, (n + 1) * D_MODEL)
        acc = jnp.dot(h, w_ref[:, cols], preferred_element_type=f32)
        if n == COL_Q:
            acc = acc * Q_SCALE
        o_ref[:, cols] = acc.astype(bf16)
    vt = lax.dot_general(wvt_ref[...], h, (((1,), (1,)), ((), ())), preferred_element_type=f32)
    vt_ref[0] = vt.astype(bf16)


def _in_proj(x2d, norm_g, w_main, w_vt, batch):
    t = x2d.shape[0]
    d_main = w_main.shape[1]
    tiles_per_batch = t // batch // TM_PROJ
    return pl.pallas_call(
        _inproj_kernel,
        grid=(t // TM_PROJ,),
        in_specs=[
            pl.BlockSpec((TM_PROJ, D_MODEL), lambda i: (i, 0)),
            _const_spec((1, D_MODEL)),
            _const_spec((D_MODEL, d_main)),
            _const_spec((D_MODEL, D_MODEL)),
        ],
        out_specs=[
            pl.BlockSpec((TM_PROJ, d_main), lambda i: (i, 0)),
            pl.BlockSpec((1, D_MODEL, TM_PROJ), lambda i: (i // tiles_per_batch, 0, i % tiles_per_batch)),
        ],
        out_shape=[
            jax.ShapeDtypeStruct((t, d_main), bf16),
            jax.ShapeDtypeStruct((batch, D_MODEL, t // batch), bf16),
        ],
        compiler_params=pltpu.CompilerParams(
            dimension_semantics=("arbitrary",), vmem_limit_bytes=VMEM_LIMIT_BYTES),
        name="in_proj",
    )(x2d, norm_g, w_main, w_vt)


CONV_ROW_CHUNK = 64
CONV_LANE_CHUNK = LANES
LN_ROW_CHUNK = 16
LN_UNROLL = 4
CONV_BASE = CONV_HALO - CONV_K // 2
CONV_MAX_ALIGNED_OFF = (CONV_BASE + CONV_K - 1) // SUBLANES * SUBLANES


def _conv_kernel(av_ref, ag_ref, za_ref, avp_ref, agp_ref, avn_ref, agn_ref,
                 dww_ref, dwb_ref, lng_ref, lnb_ref, o_ref, u_ref, c_ref):
    i = pl.program_id(1)
    n_tiles = pl.num_programs(1)
    ts = av_ref.shape[1]

    def glu(a_ref, g_ref):
        return a_ref[0].astype(f32) * _sigmoid(g_ref[0].astype(f32))

    u_ref[0, 0:CONV_HALO, :] = jnp.where(i > 0, glu(avp_ref, agp_ref), 0.0)
    u_ref[0, CONV_HALO:CONV_HALO + ts, :] = glu(av_ref, ag_ref)
    u_ref[0, CONV_HALO + ts:, :] = jnp.where(i < n_tiles - 1, glu(avn_ref, agn_ref), 0.0)

    n_shifted = ts + CONV_MAX_ALIGNED_OFF

    def conv_chunk(idx, carry):
        lanes = pl.ds(pl.multiple_of(idx * CONV_LANE_CHUNK, CONV_LANE_CHUNK), CONV_LANE_CHUNK)
        for s in range(1, SUBLANES):
            u_ref[s, 0:n_shifted, lanes] = u_ref[0, s:s + n_shifted, lanes]
        for r0 in range(0, ts, CONV_ROW_CHUNK):
            acc = jnp.broadcast_to(dwb_ref[:, lanes], (CONV_ROW_CHUNK, CONV_LANE_CHUNK))
            for k in range(CONV_K):
                a, s = divmod(CONV_BASE + k, SUBLANES)
                start = r0 + a * SUBLANES
                acc = acc + dww_ref[k:k + 1, lanes] * u_ref[s, start:start + CONV_ROW_CHUNK, lanes]
            c_ref[r0:r0 + CONV_ROW_CHUNK, lanes] = acc
        return carry

    lax.fori_loop(0, D_MODEL // CONV_LANE_CHUNK, conv_chunk, 0)

    def ln_chunk(idx, carry):
        r0 = pl.multiple_of(idx * LN_ROW_CHUNK, LN_ROW_CHUNK)
        rows = pl.ds(r0, LN_ROW_CHUNK)
        c = c_ref[rows, :]
        mu = jnp.mean(c, axis=-1, keepdims=True)
        d = c - mu
        var = jnp.mean(d * d, axis=-1, keepdims=True)
        y = d * lax.rsqrt(var + LN_EPS) * lng_ref[...] + lnb_ref[...]
        o_ref[0, rows, :] = (_silu(y) * _silu(za_ref[0, rows, :].astype(f32))).astype(bf16)
        return carry

    lax.fori_loop(0, ts // LN_ROW_CHUNK, ln_chunk, 0, unroll=LN_UNROLL)


def _conv_branch(proj3d, dw_w, dw_b, ln_g, ln_b):
    b, l, _ = proj3d.shape
    ts = TS_CONV
    halo_blocks_per_tile = ts // CONV_HALO
    n_halo_blocks = l // CONV_HALO

    def main(col):
        return pl.BlockSpec((1, ts, D_MODEL), lambda bi, i: (bi, i, col))

    def prev(col):
        return pl.BlockSpec((1, CONV_HALO, D_MODEL),
                            lambda bi, i: (bi, jnp.maximum(i * halo_blocks_per_tile - 1, 0), col))

    def nxt(col):
        return pl.BlockSpec((1, CONV_HALO, D_MODEL),
                            lambda bi, i: (bi, jnp.minimum((i + 1) * halo_blocks_per_tile, n_halo_blocks - 1), col))

    return pl.pallas_call(
        _conv_kernel,
        grid=(b, l // ts),
        in_specs=[
            main(COL_A_VAL), main(COL_A_GLU), main(COL_Z_A),
            prev(COL_A_VAL), prev(COL_A_GLU), nxt(COL_A_VAL), nxt(COL_A_GLU),
            _const_spec((CONV_K, D_MODEL)), _const_spec((1, D_MODEL)),
            _const_spec((1, D_MODEL)), _const_spec((1, D_MODEL)),
        ],
        out_specs=pl.BlockSpec((1, ts, D_MODEL), lambda bi, i: (bi, i, 0)),
        out_shape=jax.ShapeDtypeStruct((b, l, D_MODEL), bf16),
        scratch_shapes=[
            pltpu.VMEM((SUBLANES, ts + 2 * CONV_HALO, D_MODEL), f32),
            pltpu.VMEM((ts, D_MODEL), f32),
        ],
        compiler_params=pltpu.CompilerParams(
            dimension_semantics=("arbitrary", "arbitrary"), vmem_limit_bytes=VMEM_LIMIT_BYTES),
        name="conv_branch",
    )(proj3d, proj3d, proj3d, proj3d, proj3d, proj3d, proj3d, dw_w, dw_b, ln_g, ln_b)


def _bias_table(rpb):
    qc = np.arange(GRID_W)[None, :]
    kc = np.arange(GRID_W)[:, None]
    win_start = np.clip(qc - WIN_COLS // 2, 0, GRID_W - WIN_COLS)
    valid = (kc >= win_start) & (kc < win_start + WIN_COLS)
    col_idx = np.clip(kc - qc, -(WIN_COLS - 1), WIN_COLS - 1) + WIN_COLS - 1
    t = rpb.astype(f32)[:, :, col_idx] * LOG2E
    t = jnp.where(jnp.asarray(valid)[None, None], t, NEG_INF)
    n_dr = 2 * WIN_ROWS - 1
    return t.transpose(1, 2, 0, 3).reshape(n_dr * GRID_W, N_HEADS * GRID_W)


PAIR_W = 2 * GRID_W
ATTN_ROW_UNROLL = 8
assert PAIR_W == LANES


def _attn_kernel(q_ref, zb_ref, k_ref, vt_ref, t_ref, o_ref, *, n_rows):
    step = pl.program_id(1)
    seq_len = n_rows * GRID_W
    mid = (WIN_ROWS - 2) * GRID_W
    lane_head = lax.broadcasted_iota(jnp.int32, (GRID_W, GROUP_W), 1) // HEAD_DIM
    low_half = lax.broadcasted_iota(jnp.int32, (GROUP_W, PAIR_W), 1) < GRID_W

    def row_body(j, carry):
        r = step * ROWS_PER_STEP + j
        rs = jnp.clip(r - WIN_ROWS // 2, 0, n_rows - WIN_ROWS)
        odd = rs % 2
        a = rs - odd
        x0 = jnp.where(odd == 1, rs + WIN_ROWS - 1, rs)
        x1 = jnp.where(odd == 1, rs, rs + 1)
        k_x0 = pl.multiple_of(x0 * GRID_W, GRID_W)
        k_x1 = pl.multiple_of(x1 * GRID_W, GRID_W)
        k_mid = pl.multiple_of((a + 2) * GRID_W, PAIR_W)
        t_x0 = pl.multiple_of((x0 - r + WIN_ROWS - 1) * GRID_W, GRID_W)
        t_x1 = pl.multiple_of((x1 - r + WIN_ROWS - 1) * GRID_W, GRID_W)
        t_mid = pl.multiple_of((a + 2 - r + WIN_ROWS - 1) * GRID_W, GRID_W)
        v_lo = pl.multiple_of(a * GRID_W, PAIR_W)
        v_hi = pl.multiple_of(jnp.minimum((a + WIN_ROWS) * GRID_W, seq_len - PAIR_W), PAIR_W)
        take_hi = jnp.logical_and(low_half, odd == 1)
        q0 = pl.multiple_of(j * GRID_W, GRID_W)
        for g in range(N_GROUPS):
            cols = slice(g * GROUP_W, (g + 1) * GROUP_W)
            qg = q_ref[0, pl.ds(q0, GRID_W), cols]
            zero = jnp.zeros_like(qg)
            qbd = jnp.concatenate(
                [jnp.where(lane_head == h, qg, zero) for h in range(HEADS_PER_GROUP)], axis=0)
            kw = jnp.concatenate([k_ref[0, pl.ds(k_x0, GRID_W), cols],
                                  k_ref[0, pl.ds(k_x1, GRID_W), cols],
                                  k_ref[0, pl.ds(k_mid, mid), cols]], axis=0)
            tb = jnp.concatenate([t_ref[pl.ds(t_x0, GRID_W), cols],
                                  t_ref[pl.ds(t_x1, GRID_W), cols],
                                  t_ref[pl.ds(t_mid, mid), cols]], axis=0)
            s = lax.dot_general(kw, qbd, (((1,), (1,)), ((), ())),
                                preferred_element_type=f32) + tb
            m = jnp.max(s, axis=0, keepdims=True)
            e = jnp.exp2(s - m)
            rinv = 1.0 / jnp.sum(e, axis=0, keepdims=True)
            v_first = jnp.where(take_hi, vt_ref[0, cols, pl.ds(v_hi, PAIR_W)],
                                vt_ref[0, cols, pl.ds(v_lo, PAIR_W)])
            vtw = jnp.concatenate([v_first, vt_ref[0, cols, pl.ds(k_mid, mid)]], axis=1)
            o2t = jnp.dot(vtw, e.astype(bf16), preferred_element_type=f32)
            z = jnp.concatenate([o2t[0:LANES, 0:LANES] * rinv[:, 0:LANES],
                                 o2t[LANES:, LANES:] * rinv[:, LANES:]], axis=0)
            zt = z.T
            og = jnp.where(lane_head % 2 == 0, zt[0:GRID_W, :], zt[GRID_W:, :])
            zb = zb_ref[0, pl.ds(q0, GRID_W), cols].astype(f32)
            o_ref[0, pl.ds(q0, GRID_W), cols] = (og * _silu(zb)).astype(bf16)
        return carry

    lax.fori_loop(0, ROWS_PER_STEP, row_body, 0, unroll=ATTN_ROW_UNROLL)


def _attn_branch(proj3d, vt, bias_table):
    b, l, _ = proj3d.shape
    n_rows = l // GRID_W
    tq = ROWS_PER_STEP * GRID_W

    def tile(col):
        return pl.BlockSpec((1, tq, D_MODEL), lambda bi, i: (bi, i, col))

    return pl.pallas_call(
        functools.partial(_attn_kernel, n_rows=n_rows),
        grid=(b, n_rows // ROWS_PER_STEP),
        in_specs=[tile(COL_Q), tile(COL_Z_B),
                  pl.BlockSpec((1, l, D_MODEL), lambda bi, i: (bi, 0, COL_K), pipeline_mode=pl.Buffered(1)),
                  pl.BlockSpec((1, D_MODEL, l), lambda bi, i: (bi, 0, 0), pipeline_mode=pl.Buffered(1)),
                  _const_spec(bias_table.shape)],
        out_specs=pl.BlockSpec((1, tq, D_MODEL), lambda bi, i: (bi, i, 0)),
        out_shape=jax.ShapeDtypeStruct((b, l, D_MODEL), bf16),
        compiler_params=pltpu.CompilerParams(
            dimension_semantics=("arbitrary", "arbitrary"), vmem_limit_bytes=VMEM_LIMIT_BYTES),
        name="attn_branch",
    )(proj3d, proj3d, proj3d, vt, bias_table)


def _merge_kernel(x_ref, sa_ref, sb_ref, ga_ref, gb_ref, bg_ref, wa_ref, wb_ref, wo_ref, fg_ref, o_ref):
    y_a = jnp.dot(sa_ref[...], wa_ref[...], preferred_element_type=f32)
    y_b = jnp.dot(sb_ref[...], wb_ref[...], preferred_element_type=f32)
    g_a = _sigmoid(ga_ref[...].astype(f32) + bg_ref[:, :D_MODEL])
    g_b = _sigmoid(gb_ref[...].astype(f32) + bg_ref[:, D_MODEL:])
    merged = (g_a * y_a + g_b * y_b).astype(bf16)
    out = x_ref[...] + jnp.dot(merged, wo_ref[...], preferred_element_type=f32)
    ms = jnp.mean(out * out, axis=-1, keepdims=True)
    o_ref[...] = out * lax.rsqrt(ms + RMS_EPS) * fg_ref[...]


def _merge(x2d, s_a, s_b, proj2d, b_gate, w_pw_a, w_o_b, w_out, final_g):
    t = x2d.shape[0]
    tm = TM_MERGE

    def tile(col=0):
        return pl.BlockSpec((tm, D_MODEL), lambda i: (i, col))

    return pl.pallas_call(
        _merge_kernel,
        grid=(t // tm,),
        in_specs=[tile(), tile(), tile(), tile(COL_G_A), tile(COL_G_B),
                  _const_spec((1, 2 * D_MODEL)),
                  _const_spec((D_MODEL, D_MODEL)), _const_spec((D_MODEL, D_MODEL)),
                  _const_spec((D_MODEL, D_MODEL)), _const_spec((1, D_MODEL))],
        out_specs=tile(),
        out_shape=jax.ShapeDtypeStruct((t, D_MODEL), f32),
        compiler_params=pltpu.CompilerParams(
            dimension_semantics=("arbitrary",), vmem_limit_bytes=VMEM_LIMIT_BYTES),
        name="merge_out",
    )(x2d, s_a, s_b, proj2d, proj2d, b_gate, w_pw_a, w_o_b, w_out, final_g)


def _trunk(x, norm_g, w_in, b_gate, dw_w, dw_b, ln_g, ln_b, w_pw_a, rpb, w_o_b, w_out, final_g):
    assert norm_g.shape[0] == 1, "one layer"
    b, l, d = x.shape
    assert d == D_MODEL and l % (ROWS_PER_STEP * GRID_W) == 0 and l % TS_CONV == 0 and l % TM_PROJ == 0
    assert (b * l) % TM_MERGE == 0 and l // GRID_W >= WIN_ROWS + 2
    row = lambda v: v.reshape(1, -1)
    x2d = x.reshape(b * l, d)
    w = w_in[0].astype(bf16)
    v_cols = slice(W_IN_V_CHUNK * D_MODEL, (W_IN_V_CHUNK + 1) * D_MODEL)
    w_main = jnp.concatenate([w[:, :v_cols.start], w[:, v_cols.stop:]], axis=1)
    proj2d, vt = _in_proj(x2d, row(norm_g[0]), w_main, w[:, v_cols].T, b)
    proj3d = proj2d.reshape(b, l, -1)
    s_a = _conv_branch(proj3d, dw_w[0], row(dw_b[0]), row(ln_g[0]), row(ln_b[0]))
    s_b = _attn_branch(proj3d, vt, _bias_table(rpb[0]))
    y = _merge(x2d, s_a.reshape(b * l, d), s_b.reshape(b * l, d), proj2d, row(b_gate[0]),
               w_pw_a[0].astype(bf16), w_o_b[0].astype(bf16), w_out[0].astype(bf16), row(final_g))
    return y.reshape(b, l, d)


def kernel(x_prompt, x_sample, norm_g, w_in, b_gate, dw_w, dw_b, ln_g, ln_b, w_pw_a, rpb, w_o_b, w_out, final_g):
    params = (norm_g, w_in, b_gate, dw_w, dw_b, ln_g, ln_b, w_pw_a, rpb, w_o_b, w_out, final_g)
    return (_trunk(x_prompt, *params), _trunk(x_sample, *params))
```

```python
import functools
import math

import numpy as np
import jax
import jax.numpy as jnp
from jax import lax
from jax.experimental import pallas as pl
from jax.experimental.pallas import tpu as pltpu

D_MODEL = 1024
GRID_W = 64
CONV_K = 31
CONV_HALO = 16
SUBLANES = 8
LANES = 128
N_HEADS = 16
HEAD_DIM = 64
WIN_ROWS = 8
WIN_COLS = 16
HEADS_PER_GROUP = 4
GROUP_W = HEADS_PER_GROUP * HEAD_DIM
N_GROUPS = N_HEADS // HEADS_PER_GROUP
W_A_VAL, W_A_GLU, W_Z_A, W_Q, W_K, W_Z_B, W_G_A, W_G_B = range(8)
W_IN_V_CHUNK = 5
N_PROJ_CHUNKS = 5
COL_Q, COL_K, COL_Z_B, COL_G_A, COL_G_B = range(N_PROJ_CHUNKS)
RMS_EPS = 1e-6
LN_EPS = 1e-5
NEG_INF = -1e30
LOG2E = math.log2(math.e)
Q_SCALE = HEAD_DIM ** -0.5 * LOG2E

VMEM_LIMIT_BYTES = 60 * 1024 * 1024

TM_PROJ = 256
ROWS_PER_STEP = 8
TM_MERGE = 512

f32 = jnp.float32
bf16 = jnp.bfloat16


def _sigmoid(x):
    return 1.0 / (1.0 + jnp.exp(-x))


def _silu(x):
    return x * _sigmoid(x)


def _const_spec(shape):
    return pl.BlockSpec(shape, lambda *_: (0,) * len(shape), pipeline_mode=pl.Buffered(1))


CONV_ROW_CHUNK = 64
LN_ROW_CHUNK = 64
CONV_BASE = CONV_HALO - CONV_K // 2
CONV_MAX_ALIGNED_OFF = (CONV_BASE + CONV_K - 1) // SUBLANES * SUBLANES
N_SHIFT_BUFS = 2
COL_TILE = 256
N_COL_TILES = D_MODEL // COL_TILE


def _proj_conv_kernel(x_ref, xn_ref, g_ref, w_ref, wvt_ref, dww_ref, dwb_ref, lng_ref, lnb_ref,
                      p_ref, vt_ref, sa_ref, u_ref, ush_ref, c_ref, za_ref, h_ref):
    i = pl.program_id(1)
    n_tiles = pl.num_programs(1) - 1
    tm = x_ref.shape[1]

    def normed(ref):
        x = ref[0]
        ms = jnp.mean(x * x, axis=-1, keepdims=True)
        return (x * lax.rsqrt(ms + RMS_EPS) * g_ref[...]).astype(bf16)

    def glu_tile(h, c0):
        a = jnp.dot(h, w_ref[:, pl.ds(W_A_VAL * D_MODEL + c0, COL_TILE)], preferred_element_type=f32)
        g = jnp.dot(h, w_ref[:, pl.ds(W_A_GLU * D_MODEL + c0, COL_TILE)], preferred_element_type=f32)
        return a * _sigmoid(g)

    def ln_chunk(r0):
        rows = pl.ds(r0, LN_ROW_CHUNK)
        c = c_ref[rows, :]
        mu = jnp.mean(c, axis=-1, keepdims=True)
        d = c - mu
        var = jnp.mean(d * d, axis=-1, keepdims=True)
        y = d * lax.rsqrt(var + LN_EPS) * lng_ref[...] + lnb_ref[...]
        sa_ref[0, rows, :] = (_silu(y) * _silu(za_ref[rows, :].astype(f32))).astype(bf16)

    @pl.when(i == 0)
    def _():
        u_ref[0:CONV_HALO, :] = jnp.zeros((CONV_HALO, D_MODEL), f32)
        c_ref[...] = jnp.zeros_like(c_ref)
        za_ref[...] = jnp.zeros_like(za_ref)
        h0 = normed(x_ref)
        for t in range(N_COL_TILES):
            u_ref[CONV_HALO:CONV_HALO + tm, t * COL_TILE:(t + 1) * COL_TILE] = glu_tile(h0, t * COL_TILE)

    @pl.when(i == n_tiles)
    def _():
        for r0 in range(0, tm, LN_ROW_CHUNK):
            ln_chunk(r0)

    @pl.when(i < n_tiles)
    def _():
        h_ref[0] = normed(x_ref)
        h_ref[1] = normed(xn_ref)
        has_next = i + 1 < n_tiles

        ln_per_tile = tm // LN_ROW_CHUNK // N_COL_TILES

        def glu_ln_step(t, carry):
            c0 = pl.multiple_of(t * COL_TILE, COL_TILE)
            u_ref[CONV_HALO + tm:, pl.ds(c0, COL_TILE)] = jnp.where(has_next, glu_tile(h_ref[1], c0), 0.0)
            for j in range(ln_per_tile):
                ln_chunk(pl.multiple_of((t * ln_per_tile + j) * LN_ROW_CHUNK, LN_ROW_CHUNK))
            return carry

        lax.fori_loop(0, N_COL_TILES, glu_ln_step, 0)

        n_shifted = tm + CONV_MAX_ALIGNED_OFF

        def conv_lanes(l0, buf):
            lanes = pl.ds(l0, LANES)
            for s in range(1, SUBLANES):
                ush_ref[buf, s - 1, 0:n_shifted, :] = u_ref[s:s + n_shifted, lanes]
            for r0 in range(0, tm, CONV_ROW_CHUNK):
                acc = jnp.broadcast_to(dwb_ref[:, lanes], (CONV_ROW_CHUNK, LANES))
                for s in range(SUBLANES):
                    rows = slice(r0, r0 + CONV_ROW_CHUNK + CONV_MAX_ALIGNED_OFF)
                    blk = u_ref[rows, lanes] if s == 0 else ush_ref[buf, s - 1, rows, :]
                    for k in range(CONV_K):
                        a, sk = divmod(CONV_BASE + k, SUBLANES)
                        if sk == s:
                            acc = acc + dww_ref[k:k + 1, lanes] * blk[a * SUBLANES:a * SUBLANES + CONV_ROW_CHUNK, :]
                c_ref[r0:r0 + CONV_ROW_CHUNK, lanes] = acc

        def proj_conv_step(t, carry):
            c0 = pl.multiple_of(t * COL_TILE, COL_TILE)
            h = h_ref[0]

            def proj(chunk):
                return jnp.dot(h, w_ref[:, pl.ds(chunk * D_MODEL + c0, COL_TILE)], preferred_element_type=f32)

            p_ref[0, :, pl.ds(COL_Q * D_MODEL + c0, COL_TILE)] = (proj(W_Q) * Q_SCALE).astype(bf16)
            p_ref[0, :, pl.ds(COL_K * D_MODEL + c0, COL_TILE)] = proj(W_K).astype(bf16)
            conv_lanes(c0, 0)
            vt = lax.dot_general(wvt_ref[pl.ds(c0, COL_TILE), :], h, (((1,), (1,)), ((), ())),
                                 preferred_element_type=f32)
            vt_ref[0, pl.ds(c0, COL_TILE), :] = vt.astype(bf16)
            p_ref[0, :, pl.ds(COL_Z_B * D_MODEL + c0, COL_TILE)] = proj(W_Z_B).astype(bf16)
            p_ref[0, :, pl.ds(COL_G_A * D_MODEL + c0, COL_TILE)] = proj(W_G_A).astype(bf16)
            conv_lanes(pl.multiple_of(c0 + LANES, LANES), N_SHIFT_BUFS - 1)
            p_ref[0, :, pl.ds(COL_G_B * D_MODEL + c0, COL_TILE)] = proj(W_G_B).astype(bf16)
            za_ref[:, pl.ds(c0, COL_TILE)] = proj(W_Z_A).astype(bf16)
            return carry

        lax.fori_loop(0, N_COL_TILES, proj_conv_step, 0)

        for r0 in range(0, tm + CONV_HALO, CONV_ROW_CHUNK):
            n = min(CONV_ROW_CHUNK, tm + CONV_HALO - r0)
            u_ref[r0:r0 + n, :] = u_ref[tm + r0:tm + r0 + n, :]


def _proj_conv(x, norm_g, w_main, w_vt, dw_w, dw_b, ln_g, ln_b):
    b, l, _ = x.shape
    tm = TM_PROJ
    n_tiles = l // tm
    last = n_tiles - 1
    tile = lambda bi, i: (bi, jnp.minimum(i, last), 0)
    return pl.pallas_call(
        _proj_conv_kernel,
        grid=(b, n_tiles + 1),
        in_specs=[
            pl.BlockSpec((1, tm, D_MODEL), tile),
            pl.BlockSpec((1, tm, D_MODEL), lambda bi, i: (bi, jnp.minimum(i + 1, last), 0)),
            _const_spec((1, D_MODEL)),
            _const_spec(w_main.shape),
            _const_spec((D_MODEL, D_MODEL)),
            _const_spec((CONV_K, D_MODEL)), _const_spec((1, D_MODEL)),
            _const_spec((1, D_MODEL)), _const_spec((1, D_MODEL)),
        ],
        out_specs=[
            pl.BlockSpec((1, tm, N_PROJ_CHUNKS * D_MODEL), tile),
            pl.BlockSpec((1, D_MODEL, tm), lambda bi, i: (bi, 0, jnp.minimum(i, last))),
            pl.BlockSpec((1, tm, D_MODEL), lambda bi, i: (bi, jnp.maximum(i - 1, 0), 0)),
        ],
        out_shape=[
            jax.ShapeDtypeStruct((b, l, N_PROJ_CHUNKS * D_MODEL), bf16),
            jax.ShapeDtypeStruct((b, D_MODEL, l), bf16),
            jax.ShapeDtypeStruct((b, l, D_MODEL), bf16),
        ],
        scratch_shapes=[
            pltpu.VMEM((CONV_HALO + 2 * tm, D_MODEL), f32),
            pltpu.VMEM((N_SHIFT_BUFS, SUBLANES - 1, tm + CONV_MAX_ALIGNED_OFF, LANES), f32),
            pltpu.VMEM((tm, D_MODEL), f32),
            pltpu.VMEM((tm, D_MODEL), bf16),
            pltpu.VMEM((2, tm, D_MODEL), bf16),
        ],
        compiler_params=pltpu.CompilerParams(
            dimension_semantics=("arbitrary", "arbitrary"), vmem_limit_bytes=VMEM_LIMIT_BYTES),
        name="proj_conv",
    )(x, x, norm_g, w_main, w_vt, dw_w, dw_b, ln_g, ln_b)


def _bias_table(rpb):
    qc = np.arange(GRID_W)[None, :]
    kc = np.arange(GRID_W)[:, None]
    win_start = np.clip(qc - WIN_COLS // 2, 0, GRID_W - WIN_COLS)
    valid = (kc >= win_start) & (kc < win_start + WIN_COLS)
    col_idx = np.clip(kc - qc, -(WIN_COLS - 1), WIN_COLS - 1) + WIN_COLS - 1
    t = rpb.astype(f32)[:, :, col_idx] * LOG2E
    t = jnp.where(jnp.asarray(valid)[None, None], t, NEG_INF)
    n_dr = 2 * WIN_ROWS - 1
    return t.transpose(1, 2, 0, 3).reshape(n_dr * GRID_W, N_HEADS * GRID_W)


PAIR_W = 2 * GRID_W
ATTN_ROW_UNROLL = 8
assert PAIR_W == LANES


def _attn_kernel(q_ref, zb_ref, k_ref, vt_ref, t_ref, o_ref, *, n_rows):
    step = pl.program_id(1)
    seq_len = n_rows * GRID_W
    mid = (WIN_ROWS - 2) * GRID_W
    lane_head = lax.broadcasted_iota(jnp.int32, (GRID_W, GROUP_W), 1) // HEAD_DIM
    low_half = lax.broadcasted_iota(jnp.int32, (GROUP_W, PAIR_W), 1) < GRID_W

    def row_body(j, carry):
        r = step * ROWS_PER_STEP + j
        rs = jnp.clip(r - WIN_ROWS // 2, 0, n_rows - WIN_ROWS)
        odd = rs % 2
        a = rs - odd
        x0 = jnp.where(odd == 1, rs + WIN_ROWS - 1, rs)
        x1 = jnp.where(odd == 1, rs, rs + 1)
        k_x0 = pl.multiple_of(x0 * GRID_W, GRID_W)
        k_x1 = pl.multiple_of(x1 * GRID_W, GRID_W)
        k_mid = pl.multiple_of((a + 2) * GRID_W, PAIR_W)
        t_x0 = pl.multiple_of((x0 - r + WIN_ROWS - 1) * GRID_W, GRID_W)
        t_x1 = pl.multiple_of((x1 - r + WIN_ROWS - 1) * GRID_W, GRID_W)
        t_mid = pl.multiple_of((a + 2 - r + WIN_ROWS - 1) * GRID_W, GRID_W)
        v_lo = pl.multiple_of(a * GRID_W, PAIR_W)
        v_hi = pl.multiple_of(jnp.minimum((a + WIN_ROWS) * GRID_W, seq_len - PAIR_W), PAIR_W)
        take_hi = jnp.logical_and(low_half, odd == 1)
        q0 = pl.multiple_of(j * GRID_W, GRID_W)
        for g in range(N_GROUPS):
            cols = slice(g * GROUP_W, (g + 1) * GROUP_W)
            qg = q_ref[0, pl.ds(q0, GRID_W), cols]
            zero = jnp.zeros_like(qg)
            qbd = jnp.concatenate(
                [jnp.where(lane_head == h, qg, zero) for h in range(HEADS_PER_GROUP)], axis=0)
            kw = jnp.concatenate([k_ref[0, pl.ds(k_x0, GRID_W), cols],
                                  k_ref[0, pl.ds(k_x1, GRID_W), cols],
                                  k_ref[0, pl.ds(k_mid, mid), cols]], axis=0)
            tb = jnp.concatenate([t_ref[pl.ds(t_x0, GRID_W), cols],
                                  t_ref[pl.ds(t_x1, GRID_W), cols],
                                  t_ref[pl.ds(t_mid, mid), cols]], axis=0)
            s = lax.dot_general(kw, qbd, (((1,), (1,)), ((), ())),
                                preferred_element_type=f32) + tb
            m = jnp.max(s, axis=0, keepdims=True)
            e = jnp.exp2(s - m)
            rinv = 1.0 / jnp.sum(e, axis=0, keepdims=True)
            v_first = jnp.where(take_hi, vt_ref[0, cols, pl.ds(v_hi, PAIR_W)],
                                vt_ref[0, cols, pl.ds(v_lo, PAIR_W)])
            vtw = jnp.concatenate([v_first, vt_ref[0, cols, pl.ds(k_mid, mid)]], axis=1)
            o2t = jnp.dot(vtw, e.astype(bf16), preferred_element_type=f32)
            z = jnp.concatenate([o2t[0:LANES, 0:LANES] * rinv[:, 0:LANES],
                                 o2t[LANES:, LANES:] * rinv[:, LANES:]], axis=0)
            zt = z.T
            og = jnp.where(lane_head % 2 == 0, zt[0:GRID_W, :], zt[GRID_W:, :])
            zb = zb_ref[0, pl.ds(q0, GRID_W), cols].astype(f32)
            o_ref[0, pl.ds(q0, GRID_W), cols] = (og * _silu(zb)).astype(bf16)
        return carry

    lax.fori_loop(0, ROWS_PER_STEP, row_body, 0, unroll=ATTN_ROW_UNROLL)


def _attn_branch(proj3d, vt, bias_table):
    b, l, _ = proj3d.shape
    n_rows = l // GRID_W
    tq = ROWS_PER_STEP * GRID_W

    def tile(col):
        return pl.BlockSpec((1, tq, D_MODEL), lambda bi, i: (bi, i, col))

    return pl.pallas_call(
        functools.partial(_attn_kernel, n_rows=n_rows),
        grid=(b, n_rows // ROWS_PER_STEP),
        in_specs=[tile(COL_Q), tile(COL_Z_B),
                  pl.BlockSpec((1, l, D_MODEL), lambda bi, i: (bi, 0, COL_K), pipeline_mode=pl.Buffered(1)),
                  pl.BlockSpec((1, D_MODEL, l), lambda bi, i: (bi, 0, 0), pipeline_mode=pl.Buffered(1)),
                  _const_spec(bias_table.shape)],
        out_specs=pl.BlockSpec((1, tq, D_MODEL), lambda bi, i: (bi, i, 0)),
        out_shape=jax.ShapeDtypeStruct((b, l, D_MODEL), bf16),
        compiler_params=pltpu.CompilerParams(
            dimension_semantics=("arbitrary", "arbitrary"), vmem_limit_bytes=VMEM_LIMIT_BYTES),
        name="attn_branch",
    )(proj3d, proj3d, proj3d, vt, bias_table)


def _merge_kernel(x_ref, sa_ref, sb_ref, ga_ref, gb_ref, bg_ref, wa_ref, wb_ref, wo_ref, fg_ref, o_ref):
    y_a = jnp.dot(sa_ref[...], wa_ref[...], preferred_element_type=f32)
    y_b = jnp.dot(sb_ref[...], wb_ref[...], preferred_element_type=f32)
    g_a = _sigmoid(ga_ref[...].astype(f32) + bg_ref[:, :D_MODEL])
    g_b = _sigmoid(gb_ref[...].astype(f32) + bg_ref[:, D_MODEL:])
    merged = (g_a * y_a + g_b * y_b).astype(bf16)
    out = x_ref[...] + jnp.dot(merged, wo_ref[...], preferred_element_type=f32)
    ms = jnp.mean(out * out, axis=-1, keepdims=True)
    o_ref[...] = out * lax.rsqrt(ms + RMS_EPS) * fg_ref[...]


def _merge(x2d, s_a, s_b, proj2d, b_gate, w_pw_a, w_o_b, w_out, final_g):
    t = x2d.shape[0]
    tm = TM_MERGE

    def tile(col=0):
        return pl.BlockSpec((tm, D_MODEL), lambda i: (i, col))

    return pl.pallas_call(
        _merge_kernel,
        grid=(t // tm,),
        in_specs=[tile(), tile(), tile(), tile(COL_G_A), tile(COL_G_B),
                  _const_spec((1, 2 * D_MODEL)),
                  _const_spec((D_MODEL, D_MODEL)), _const_spec((D_MODEL, D_MODEL)),
                  _const_spec((D_MODEL, D_MODEL)), _const_spec((1, D_MODEL))],
        out_specs=tile(),
        out_shape=jax.ShapeDtypeStruct((t, D_MODEL), f32),
        compiler_params=pltpu.CompilerParams(
            dimension_semantics=("arbitrary",), vmem_limit_bytes=VMEM_LIMIT_BYTES),
        name="merge_out",
    )(x2d, s_a, s_b, proj2d, proj2d, b_gate, w_pw_a, w_o_b, w_out, final_g)


def _trunk(x, norm_g, w_in, b_gate, dw_w, dw_b, ln_g, ln_b, w_pw_a, rpb, w_o_b, w_out, final_g):
    assert norm_g.shape[0] == 1, "one layer"
    b, l, d = x.shape
    assert d == D_MODEL and l % (ROWS_PER_STEP * GRID_W) == 0 and l % TM_PROJ == 0
    assert (b * l) % TM_MERGE == 0 and l // GRID_W >= WIN_ROWS + 2
    row = lambda v: v.reshape(1, -1)
    w = w_in[0].astype(bf16)
    v_cols = slice(W_IN_V_CHUNK * D_MODEL, (W_IN_V_CHUNK + 1) * D_MODEL)
    w_main = jnp.concatenate([w[:, :v_cols.start], w[:, v_cols.stop:]], axis=1)
    proj3d, vt, s_a = _proj_conv(x, row(norm_g[0]), w_main, w[:, v_cols].T,
                                 dw_w[0], row(dw_b[0]), row(ln_g[0]), row(ln_b[0]))
    s_b = _attn_branch(proj3d, vt, _bias_table(rpb[0]))
    y = _merge(x.reshape(b * l, d), s_a.reshape(b * l, d), s_b.reshape(b * l, d),
               proj3d.reshape(b * l, -1), row(b_gate[0]),
               w_pw_a[0].astype(bf16), w_o_b[0].astype(bf16), w_out[0].astype(bf16), row(final_g))
    return y.reshape(b, l, d)


def kernel(x_prompt, x_sample, norm_g, w_in, b_gate, dw_w, dw_b, ln_g, ln_b, w_pw_a, rpb, w_o_b, w_out, final_g):
    params = (norm_g, w_in, b_gate, dw_w, dw_b, ln_g, ln_b, w_pw_a, rpb, w_o_b, w_out, final_g)
    return (_trunk(x_prompt, *params), _trunk(x_sample, *params))
```

```python
import functools
import math

import numpy as np
import jax
import jax.numpy as jnp
from jax import lax
from jax.experimental import pallas as pl
from jax.experimental.pallas import tpu as pltpu

D_MODEL = 1024
GRID_W = 64
CONV_K = 31
CONV_HALO = 16
SUBLANES = 8
LANES = 128
N_HEADS = 16
HEAD_DIM = 64
WIN_ROWS = 8
WIN_COLS = 16
HEADS_PER_GROUP = 4
GROUP_W = HEADS_PER_GROUP * HEAD_DIM
N_GROUPS = N_HEADS // HEADS_PER_GROUP
N_PROJ_CHUNKS = 8
COL_A_VAL, COL_A_GLU, COL_Z_A, COL_Q, COL_K, COL_Z_B, COL_G_A, COL_G_B = range(N_PROJ_CHUNKS)
W_IN_V_CHUNK = 5
RMS_EPS = 1e-6
LN_EPS = 1e-5
NEG_INF = -1e30
LOG2E = math.log2(math.e)
Q_SCALE = HEAD_DIM ** -0.5 * LOG2E

VMEM_LIMIT_BYTES = 56 * 1024 * 1024

TM_PROJ = 512
TS_CONV = 512
ROWS_PER_STEP = 8
TM_MERGE = 512

f32 = jnp.float32
bf16 = jnp.bfloat16


def _sigmoid(x):
    return jax.nn.sigmoid(x)


def _silu(x):
    return x * _sigmoid(x)


def _const_spec(shape):
    return pl.BlockSpec(shape, lambda *_: (0,) * len(shape), pipeline_mode=pl.Buffered(1))


def _inproj_kernel(x_ref, g_ref, w_ref, wvt_ref, o_ref, vt_ref):
    x = x_ref[...]
    ms = jnp.mean(x * x, axis=-1, keepdims=True)
    h = (x * lax.rsqrt(ms + RMS_EPS) * g_ref[...]).astype(bf16)
    for n in range(N_PROJ_CHUNKS):
        cols = slice(n * D_MODEL, (n + 1) * D_MODEL)
        src = n if n < W_IN_V_CHUNK else n + 1
        acc = jnp.dot(h, w_ref[:, src * D_MODEL:(src + 1) * D_MODEL], preferred_element_type=f32)
        if n == COL_Q:
            acc = acc * Q_SCALE
        o_ref[:, cols] = acc.astype(bf16)
    vt = lax.dot_general(wvt_ref[...], h, (((1,), (1,)), ((), ())), preferred_element_type=f32)
    vt_ref[0] = vt.astype(bf16)


def _in_proj(x2d, norm_g, w_all, w_vt, batch):
    t = x2d.shape[0]
    d_main = N_PROJ_CHUNKS * D_MODEL
    tiles_per_batch = t // batch // TM_PROJ
    return pl.pallas_call(
        _inproj_kernel,
        grid=(t // TM_PROJ,),
        in_specs=[
            pl.BlockSpec((TM_PROJ, D_MODEL), lambda i: (i, 0)),
            _const_spec((1, D_MODEL)),
            _const_spec(w_all.shape),
            _const_spec((D_MODEL, D_MODEL)),
        ],
        out_specs=[
            pl.BlockSpec((TM_PROJ, d_main), lambda i: (i, 0)),
            pl.BlockSpec((1, D_MODEL, TM_PROJ), lambda i: (i // tiles_per_batch, 0, i % tiles_per_batch)),
        ],
        out_shape=[
            jax.ShapeDtypeStruct((t, d_main), bf16),
            jax.ShapeDtypeStruct((batch, D_MODEL, t // batch), bf16),
        ],
        compiler_params=pltpu.CompilerParams(
            dimension_semantics=("arbitrary",), vmem_limit_bytes=VMEM_LIMIT_BYTES),
        name="in_proj",
    )(x2d, norm_g, w_all, w_vt)


CONV_ROW_CHUNK = 64
CONV_LANE_CHUNK = LANES
LN_ROW_CHUNK = 16
LN_UNROLL = 8
CONV_BASE = CONV_HALO - CONV_K // 2
CONV_MAX_ALIGNED_OFF = (CONV_BASE + CONV_K - 1) // SUBLANES * SUBLANES


def _conv_kernel(av_ref, ag_ref, za_ref, avp_ref, agp_ref, avn_ref, agn_ref,
                 dww_ref, dwb_ref, lng_ref, lnb_ref, o_ref, u_ref, c_ref):
    i = pl.program_id(1)
    n_tiles = pl.num_programs(1)
    ts = av_ref.shape[1]

    def glu(a_ref, g_ref):
        return a_ref[0].astype(f32) * _sigmoid(g_ref[0].astype(f32))

    u_ref[0, 0:CONV_HALO, :] = jnp.where(i > 0, glu(avp_ref, agp_ref), 0.0)
    u_ref[0, CONV_HALO:CONV_HALO + ts, :] = glu(av_ref, ag_ref)
    u_ref[0, CONV_HALO + ts:, :] = jnp.where(i < n_tiles - 1, glu(avn_ref, agn_ref), 0.0)

    n_shifted = ts + CONV_MAX_ALIGNED_OFF

    def conv_chunk(idx, carry):
        lanes = pl.ds(pl.multiple_of(idx * CONV_LANE_CHUNK, CONV_LANE_CHUNK), CONV_LANE_CHUNK)
        for s in range(1, SUBLANES):
            u_ref[s, 0:n_shifted, lanes] = u_ref[0, s:s + n_shifted, lanes]
        for r0 in range(0, ts, CONV_ROW_CHUNK):
            acc = jnp.broadcast_to(dwb_ref[:, lanes], (CONV_ROW_CHUNK, CONV_LANE_CHUNK))
            for s in range(SUBLANES):
                blk = u_ref[s, r0:r0 + CONV_ROW_CHUNK + CONV_MAX_ALIGNED_OFF, lanes]
                for k in range(CONV_K):
                    a, sk = divmod(CONV_BASE + k, SUBLANES)
                    if sk == s:
                        acc = acc + dww_ref[k:k + 1, lanes] * blk[a * SUBLANES:a * SUBLANES + CONV_ROW_CHUNK, :]
            c_ref[r0:r0 + CONV_ROW_CHUNK, lanes] = acc
        return carry

    lax.fori_loop(0, D_MODEL // CONV_LANE_CHUNK, conv_chunk, 0)

    def ln_chunk(idx, carry):
        r0 = pl.multiple_of(idx * LN_ROW_CHUNK, LN_ROW_CHUNK)
        rows = pl.ds(r0, LN_ROW_CHUNK)
        c = c_ref[rows, :]
        mu = jnp.mean(c, axis=-1, keepdims=True)
        d = c - mu
        var = jnp.mean(d * d, axis=-1, keepdims=True)
        y = d * lax.rsqrt(var + LN_EPS) * lng_ref[...] + lnb_ref[...]
        o_ref[0, rows, :] = (_silu(y) * _silu(za_ref[0, rows, :].astype(f32))).astype(bf16)
        return carry

    lax.fori_loop(0, ts // LN_ROW_CHUNK, ln_chunk, 0, unroll=LN_UNROLL)


def _conv_branch(proj3d, dw_w, dw_b, ln_g, ln_b):
    b, l, _ = proj3d.shape
    ts = TS_CONV
    halo_blocks_per_tile = ts // CONV_HALO
    n_halo_blocks = l // CONV_HALO

    def main(col):
        return pl.BlockSpec((1, ts, D_MODEL), lambda bi, i: (bi, i, col))

    def prev(col):
        return pl.BlockSpec((1, CONV_HALO, D_MODEL),
                            lambda bi, i: (bi, jnp.maximum(i * halo_blocks_per_tile - 1, 0), col))

    def nxt(col):
        return pl.BlockSpec((1, CONV_HALO, D_MODEL),
                            lambda bi, i: (bi, jnp.minimum((i + 1) * halo_blocks_per_tile, n_halo_blocks - 1), col))

    return pl.pallas_call(
        _conv_kernel,
        grid=(b, l // ts),
        in_specs=[
            main(COL_A_VAL), main(COL_A_GLU), main(COL_Z_A),
            prev(COL_A_VAL), prev(COL_A_GLU), nxt(COL_A_VAL), nxt(COL_A_GLU),
            _const_spec((CONV_K, D_MODEL)), _const_spec((1, D_MODEL)),
            _const_spec((1, D_MODEL)), _const_spec((1, D_MODEL)),
        ],
        out_specs=pl.BlockSpec((1, ts, D_MODEL), lambda bi, i: (bi, i, 0)),
        out_shape=jax.ShapeDtypeStruct((b, l, D_MODEL), bf16),
        scratch_shapes=[
            pltpu.VMEM((SUBLANES, ts + 2 * CONV_HALO, D_MODEL), f32),
            pltpu.VMEM((ts, D_MODEL), f32),
        ],
        compiler_params=pltpu.CompilerParams(
            dimension_semantics=("arbitrary", "arbitrary"), vmem_limit_bytes=VMEM_LIMIT_BYTES),
        name="conv_branch",
    )(proj3d, proj3d, proj3d, proj3d, proj3d, proj3d, proj3d, dw_w, dw_b, ln_g, ln_b)


def _bias_table(rpb):
    qc = np.arange(GRID_W)[None, :]
    kc = np.arange(GRID_W)[:, None]
    win_start = np.clip(qc - WIN_COLS // 2, 0, GRID_W - WIN_COLS)
    valid = (kc >= win_start) & (kc < win_start + WIN_COLS)
    col_idx = np.clip(kc - qc, -(WIN_COLS - 1), WIN_COLS - 1) + WIN_COLS - 1
    t = rpb.astype(f32)[:, :, col_idx] * LOG2E
    t = jnp.where(jnp.asarray(valid)[None, None], t, NEG_INF)
    n_dr = 2 * WIN_ROWS - 1
    return t.transpose(1, 2, 0, 3).reshape(n_dr * GRID_W, N_HEADS * GRID_W)


PAIR_W = 2 * GRID_W
ATTN_ROW_UNROLL = 8
assert PAIR_W == LANES


def _attn_kernel(q_ref, zb_ref, k_ref, vt_ref, t_ref, o_ref, *, n_rows):
    step = pl.program_id(1)
    seq_len = n_rows * GRID_W
    mid = (WIN_ROWS - 2) * GRID_W
    lane_head = lax.broadcasted_iota(jnp.int32, (GRID_W, GROUP_W), 1) // HEAD_DIM
    low_half = lax.broadcasted_iota(jnp.int32, (GROUP_W, PAIR_W), 1) < GRID_W

    def row_body(j, carry):
        r = step * ROWS_PER_STEP + j
        rs = jnp.clip(r - WIN_ROWS // 2, 0, n_rows - WIN_ROWS)
        odd = rs % 2
        a = rs - odd
        x0 = jnp.where(odd == 1, rs + WIN_ROWS - 1, rs)
        x1 = jnp.where(odd == 1, rs, rs + 1)
        k_x0 = pl.multiple_of(x0 * GRID_W, GRID_W)
        k_x1 = pl.multiple_of(x1 * GRID_W, GRID_W)
        k_mid = pl.multiple_of((a + 2) * GRID_W, PAIR_W)
        t_x0 = pl.multiple_of((x0 - r + WIN_ROWS - 1) * GRID_W, GRID_W)
        t_x1 = pl.multiple_of((x1 - r + WIN_ROWS - 1) * GRID_W, GRID_W)
        t_mid = pl.multiple_of((a + 2 - r + WIN_ROWS - 1) * GRID_W, GRID_W)
        v_lo = pl.multiple_of(a * GRID_W, PAIR_W)
        v_hi = pl.multiple_of(jnp.minimum((a + WIN_ROWS) * GRID_W, seq_len - PAIR_W), PAIR_W)
        take_hi = jnp.logical_and(low_half, odd == 1)
        q0 = pl.multiple_of(j * GRID_W, GRID_W)
        for g in range(N_GROUPS):
            cols = slice(g * GROUP_W, (g + 1) * GROUP_W)
            qg = q_ref[0, pl.ds(q0, GRID_W), cols]
            zero = jnp.zeros_like(qg)
            qbd = jnp.concatenate(
                [jnp.where(lane_head == h, qg, zero) for h in range(HEADS_PER_GROUP)], axis=0)
            kw = jnp.concatenate([k_ref[0, pl.ds(k_x0, GRID_W), cols],
                                  k_ref[0, pl.ds(k_x1, GRID_W), cols],
                                  k_ref[0, pl.ds(k_mid, mid), cols]], axis=0)
            tb = jnp.concatenate([t_ref[pl.ds(t_x0, GRID_W), cols],
                                  t_ref[pl.ds(t_x1, GRID_W), cols],
                                  t_ref[pl.ds(t_mid, mid), cols]], axis=0)
            s = lax.dot_general(kw, qbd, (((1,), (1,)), ((), ())),
                                preferred_element_type=f32) + tb
            m = jnp.max(s, axis=0, keepdims=True)
            e = jnp.exp2(s - m)
            rinv = 1.0 / jnp.sum(e, axis=0, keepdims=True)
            v_first = jnp.where(take_hi, vt_ref[0, cols, pl.ds(v_hi, PAIR_W)],
                                vt_ref[0, cols, pl.ds(v_lo, PAIR_W)])
            vtw = jnp.concatenate([v_first, vt_ref[0, cols, pl.ds(k_mid, mid)]], axis=1)
            o2t = jnp.dot(vtw, e.astype(bf16), preferred_element_type=f32)
            z = jnp.concatenate([o2t[0:LANES, 0:LANES] * rinv[:, 0:LANES],
                                 o2t[LANES:, LANES:] * rinv[:, LANES:]], axis=0)
            zt = z.T
            og = jnp.where(lane_head % 2 == 0, zt[0:GRID_W, :], zt[GRID_W:, :])
            zb = zb_ref[0, pl.ds(q0, GRID_W), cols].astype(f32)
            o_ref[0, pl.ds(q0, GRID_W), cols] = (og * _silu(zb)).astype(bf16)
        return carry

    lax.fori_loop(0, ROWS_PER_STEP, row_body, 0, unroll=ATTN_ROW_UNROLL)


def _attn_branch(proj3d, vt, bias_table):
    b, l, _ = proj3d.shape
    n_rows = l // GRID_W
    tq = ROWS_PER_STEP * GRID_W

    def tile(col):
        return pl.BlockSpec((1, tq, D_MODEL), lambda bi, i: (bi, i, col))

    return pl.pallas_call(
        functools.partial(_attn_kernel, n_rows=n_rows),
        grid=(b, n_rows // ROWS_PER_STEP),
        in_specs=[tile(COL_Q), tile(COL_Z_B),
                  pl.BlockSpec((1, l, D_MODEL), lambda bi, i: (bi, 0, COL_K), pipeline_mode=pl.Buffered(1)),
                  pl.BlockSpec((1, D_MODEL, l), lambda bi, i: (bi, 0, 0), pipeline_mode=pl.Buffered(1)),
                  _const_spec(bias_table.shape)],
        out_specs=pl.BlockSpec((1, tq, D_MODEL), lambda bi, i: (bi, i, 0)),
        out_shape=jax.ShapeDtypeStruct((b, l, D_MODEL), bf16),
        compiler_params=pltpu.CompilerParams(
            dimension_semantics=("arbitrary", "arbitrary"), vmem_limit_bytes=VMEM_LIMIT_BYTES),
        name="attn_branch",
    )(proj3d, proj3d, proj3d, vt, bias_table)


def _merge_kernel(x_ref, sa_ref, sb_ref, ga_ref, gb_ref, bg_ref, wa_ref, wb_ref, wo_ref, fg_ref, o_ref):
    y_a = jnp.dot(sa_ref[...], wa_ref[...], preferred_element_type=f32)
    y_b = jnp.dot(sb_ref[...], wb_ref[...], preferred_element_type=f32)
    g_a = _sigmoid(ga_ref[...].astype(f32) + bg_ref[:, :D_MODEL])
    g_b = _sigmoid(gb_ref[...].astype(f32) + bg_ref[:, D_MODEL:])
    merged = (g_a * y_a + g_b * y_b).astype(bf16)
    out = x_ref[...] + jnp.dot(merged, wo_ref[...], preferred_element_type=f32)
    ms = jnp.mean(out * out, axis=-1, keepdims=True)
    o_ref[...] = out * lax.rsqrt(ms + RMS_EPS) * fg_ref[...]


def _merge(x2d, s_a, s_b, proj2d, b_gate, w_pw_a, w_o_b, w_out, final_g):
    t = x2d.shape[0]
    tm = TM_MERGE

    def tile(col=0):
        return pl.BlockSpec((tm, D_MODEL), lambda i: (i, col))

    return pl.pallas_call(
        _merge_kernel,
        grid=(t // tm,),
        in_specs=[tile(), tile(), tile(), tile(COL_G_A), tile(COL_G_B),
                  _const_spec((1, 2 * D_MODEL)),
                  _const_spec((D_MODEL, D_MODEL)), _const_spec((D_MODEL, D_MODEL)),
                  _const_spec((D_MODEL, D_MODEL)), _const_spec((1, D_MODEL))],
        out_specs=tile(),
        out_shape=jax.ShapeDtypeStruct((t, D_MODEL), f32),
        compiler_params=pltpu.CompilerParams(
            dimension_semantics=("arbitrary",), vmem_limit_bytes=VMEM_LIMIT_BYTES),
        name="merge_out",
    )(x2d, s_a, s_b, proj2d, proj2d, b_gate, w_pw_a, w_o_b, w_out, final_g)


def _trunk(x, norm_g, w_in, b_gate, dw_w, dw_b, ln_g, ln_b, w_pw_a, rpb, w_o_b, w_out, final_g):
    assert norm_g.shape[0] == 1, "one layer"
    b, l, d = x.shape
    assert d == D_MODEL and l % (ROWS_PER_STEP * GRID_W) == 0 and l % TS_CONV == 0 and l % TM_PROJ == 0
    assert (b * l) % TM_MERGE == 0 and l // GRID_W >= WIN_ROWS + 2
    row = lambda v: v.reshape(1, -1)
    x2d = x.reshape(b * l, d)
    w = w_in[0].astype(bf16)
    v_cols = slice(W_IN_V_CHUNK * D_MODEL, (W_IN_V_CHUNK + 1) * D_MODEL)
    proj2d, vt = _in_proj(x2d, row(norm_g[0]), w, w[:, v_cols].T, b)
    proj3d = proj2d.reshape(b, l, -1)
    s_a = _conv_branch(proj3d, dw_w[0], row(dw_b[0]), row(ln_g[0]), row(ln_b[0]))
    s_b = _attn_branch(proj3d, vt, _bias_table(rpb[0]))
    y = _merge(x2d, s_a.reshape(b * l, d), s_b.reshape(b * l, d), proj2d, row(b_gate[0]),
               w_pw_a[0].astype(bf16), w_o_b[0].astype(bf16), w_out[0].astype(bf16), row(final_g))
    return y.reshape(b, l, d)


def kernel(x_prompt, x_sample, norm_g, w_in, b_gate, dw_w, dw_b, ln_g, ln_b, w_pw_a, rpb, w_o_b, w_out, final_g):
    params = (norm_g, w_in, b_gate, dw_w, dw_b, ln_g, ln_b, w_pw_a, rpb, w_o_b, w_out, final_g)
    return (_trunk(x_prompt, *params), _trunk(x_sample, *params))
```

```python
import functools
import math

import numpy as np
import jax
import jax.numpy as jnp
from jax import lax
from jax.experimental import pallas as pl
from jax.experimental.pallas import tpu as pltpu

D_MODEL = 1024
GRID_W = 64
CONV_K = 31
CONV_HALO = 16
SUBLANES = 8
LANES = 128
N_HEADS = 16
HEAD_DIM = 64
WIN_ROWS = 8
WIN_COLS = 16
HEADS_PER_GROUP = 4
GROUP_W = HEADS_PER_GROUP * HEAD_DIM
N_GROUPS = N_HEADS // HEADS_PER_GROUP
N_PROJ_CHUNKS = 8
COL_A_VAL, COL_A_GLU, COL_Z_A, COL_Q, COL_K, COL_Z_B, COL_G_A, COL_G_B = range(N_PROJ_CHUNKS)
W_IN_V_CHUNK = 5
RMS_EPS = 1e-6
LN_EPS = 1e-5
NEG_INF = -1e30
LOG2E = math.log2(math.e)
Q_SCALE = HEAD_DIM ** -0.5 * LOG2E

VMEM_LIMIT_BYTES = 56 * 1024 * 1024

TM_PROJ = 512
TS_CONV = 512
ROWS_PER_STEP = 8
TM_MERGE = 512

f32 = jnp.float32
bf16 = jnp.bfloat16


def _sigmoid(x):
    return jax.nn.sigmoid(x)


def _silu(x):
    return x * _sigmoid(x)


def _const_spec(shape):
    return pl.BlockSpec(shape, lambda *_: (0,) * len(shape), pipeline_mode=pl.Buffered(1))


def _inproj_kernel(x_ref, g_ref, w_ref, wvt_ref, o_ref, vt_ref):
    x = x_ref[...]
    ms = jnp.mean(x * x, axis=-1, keepdims=True)
    h = (x * lax.rsqrt(ms + RMS_EPS) * g_ref[...]).astype(bf16)
    for n in range(N_PROJ_CHUNKS):
        cols = slice(n * D_MODEL, (n + 1) * D_MODEL)
        src = n if n < W_IN_V_CHUNK else n + 1
        acc = jnp.dot(h, w_ref[:, src * D_MODEL:(src + 1) * D_MODEL], preferred_element_type=f32)
        if n == COL_Q:
            acc = acc * Q_SCALE
        o_ref[:, cols] = acc.astype(bf16)
    vt = lax.dot_general(wvt_ref[...], h, (((1,), (1,)), ((), ())), preferred_element_type=f32)
    vt_ref[0] = vt.astype(bf16)


def _in_proj(x2d, norm_g, w_all, w_vt, batch):
    t = x2d.shape[0]
    d_main = N_PROJ_CHUNKS * D_MODEL
    tiles_per_batch = t // batch // TM_PROJ
    return pl.pallas_call(
        _inproj_kernel,
        grid=(t // TM_PROJ,),
        in_specs=[
            pl.BlockSpec((TM_PROJ, D_MODEL), lambda i: (i, 0)),
            _const_spec((1, D_MODEL)),
            _const_spec(w_all.shape),
            _const_spec((D_MODEL, D_MODEL)),
        ],
        out_specs=[
            pl.BlockSpec((TM_PROJ, d_main), lambda i: (i, 0)),
            pl.BlockSpec((1, D_MODEL, TM_PROJ), lambda i: (i // tiles_per_batch, 0, i % tiles_per_batch)),
        ],
        out_shape=[
            jax.ShapeDtypeStruct((t, d_main), bf16),
            jax.ShapeDtypeStruct((batch, D_MODEL, t // batch), bf16),
        ],
        compiler_params=pltpu.CompilerParams(
            dimension_semantics=("arbitrary",), vmem_limit_bytes=VMEM_LIMIT_BYTES),
        name="in_proj",
    )(x2d, norm_g, w_all, w_vt)


CONV_ROW_CHUNK = 64
CONV_LANE_CHUNK = LANES
LN_ROW_CHUNK = 16
LN_UNROLL = 8
CONV_BASE = CONV_HALO - CONV_K // 2
CONV_MAX_ALIGNED_OFF = (CONV_BASE + CONV_K - 1) // SUBLANES * SUBLANES


def _conv_kernel(av_ref, ag_ref, za_ref, avp_ref, agp_ref, avn_ref, agn_ref,
                 dww_ref, dwb_ref, lng_ref, lnb_ref, o_ref, u_ref, c_ref):
    i = pl.program_id(1)
    n_tiles = pl.num_programs(1)
    ts = av_ref.shape[1]

    def glu(a_ref, g_ref):
        return a_ref[0].astype(f32) * _sigmoid(g_ref[0].astype(f32))

    u_ref[0, 0:CONV_HALO, :] = jnp.where(i > 0, glu(avp_ref, agp_ref), 0.0)
    u_ref[0, CONV_HALO:CONV_HALO + ts, :] = glu(av_ref, ag_ref)
    u_ref[0, CONV_HALO + ts:, :] = jnp.where(i < n_tiles - 1, glu(avn_ref, agn_ref), 0.0)

    n_shifted = ts + CONV_MAX_ALIGNED_OFF

    def conv_chunk(idx, carry):
        lanes = pl.ds(pl.multiple_of(idx * CONV_LANE_CHUNK, CONV_LANE_CHUNK), CONV_LANE_CHUNK)
        for s in range(1, SUBLANES):
            u_ref[s, 0:n_shifted, lanes] = u_ref[0, s:s + n_shifted, lanes]
        for r0 in range(0, ts, CONV_ROW_CHUNK):
            acc = jnp.broadcast_to(dwb_ref[:, lanes], (CONV_ROW_CHUNK, CONV_LANE_CHUNK))
            for s in range(SUBLANES):
                blk = u_ref[s, r0:r0 + CONV_ROW_CHUNK + CONV_MAX_ALIGNED_OFF, lanes]
                for k in range(CONV_K):
                    a, sk = divmod(CONV_BASE + k, SUBLANES)
                    if sk == s:
                        acc = acc + dww_ref[k:k + 1, lanes] * blk[a * SUBLANES:a * SUBLANES + CONV_ROW_CHUNK, :]
            c_ref[r0:r0 + CONV_ROW_CHUNK, lanes] = acc
        return carry

    lax.fori_loop(0, D_MODEL // CONV_LANE_CHUNK, conv_chunk, 0)

    def ln_chunk(idx, carry):
        r0 = pl.multiple_of(idx * LN_ROW_CHUNK, LN_ROW_CHUNK)
        rows = pl.ds(r0, LN_ROW_CHUNK)
        c = c_ref[rows, :]
        mu = jnp.mean(c, axis=-1, keepdims=True)
        d = c - mu
        var = jnp.mean(d * d, axis=-1, keepdims=True)
        y = d * lax.rsqrt(var + LN_EPS) * lng_ref[...] + lnb_ref[...]
        o_ref[0, rows, :] = (_silu(y) * _silu(za_ref[0, rows, :].astype(f32))).astype(bf16)
        return carry

    lax.fori_loop(0, ts // LN_ROW_CHUNK, ln_chunk, 0, unroll=LN_UNROLL)


def _conv_branch(proj3d, dw_w, dw_b, ln_g, ln_b):
    b, l, _ = proj3d.shape
    ts = TS_CONV
    halo_blocks_per_tile = ts // CONV_HALO
    n_halo_blocks = l // CONV_HALO

    def main(col):
        return pl.BlockSpec((1, ts, D_MODEL), lambda bi, i: (bi, i, col))

    def prev(col):
        return pl.BlockSpec((1, CONV_HALO, D_MODEL),
                            lambda bi, i: (bi, jnp.maximum(i * halo_blocks_per_tile - 1, 0), col))

    def nxt(col):
        return pl.BlockSpec((1, CONV_HALO, D_MODEL),
                            lambda bi, i: (bi, jnp.minimum((i + 1) * halo_blocks_per_tile, n_halo_blocks - 1), col))

    return pl.pallas_call(
        _conv_kernel,
        grid=(b, l // ts),
        in_specs=[
            main(COL_A_VAL), main(COL_A_GLU), main(COL_Z_A),
            prev(COL_A_VAL), prev(COL_A_GLU), nxt(COL_A_VAL), nxt(COL_A_GLU),
            _const_spec((CONV_K, D_MODEL)), _const_spec((1, D_MODEL)),
            _const_spec((1, D_MODEL)), _const_spec((1, D_MODEL)),
        ],
        out_specs=pl.BlockSpec((1, ts, D_MODEL), lambda bi, i: (bi, i, 0)),
        out_shape=jax.ShapeDtypeStruct((b, l, D_MODEL), bf16),
        scratch_shapes=[
            pltpu.VMEM((SUBLANES, ts + 2 * CONV_HALO, D_MODEL), f32),
            pltpu.VMEM((ts, D_MODEL), f32),
        ],
        compiler_params=pltpu.CompilerParams(
            dimension_semantics=("arbitrary", "arbitrary"), vmem_limit_bytes=VMEM_LIMIT_BYTES),
        name="conv_branch",
    )(proj3d, proj3d, proj3d, proj3d, proj3d, proj3d, proj3d, dw_w, dw_b, ln_g, ln_b)


def _bias_table(rpb):
    qc = np.arange(GRID_W)[None, :]
    kc = np.arange(GRID_W)[:, None]
    win_start = np.clip(qc - WIN_COLS // 2, 0, GRID_W - WIN_COLS)
    valid = (kc >= win_start) & (kc < win_start + WIN_COLS)
    col_idx = np.clip(kc - qc, -(WIN_COLS - 1), WIN_COLS - 1) + WIN_COLS - 1
    onehot = jnp.asarray(col_idx[:, :, None] == np.arange(2 * WIN_COLS - 1)[None, None, :], f32)
    t = jnp.einsum('hdc,kqc->dkhq', rpb.astype(f32) * LOG2E, onehot, precision=lax.Precision.HIGHEST)
    t = jnp.where(jnp.asarray(valid)[None, :, None, :], t, NEG_INF)
    n_dr = 2 * WIN_ROWS - 1
    return t.reshape(n_dr * GRID_W, N_HEADS * GRID_W)


PAIR_W = 2 * GRID_W
ATTN_ROW_UNROLL = 8
KV_RESIDENT_BUDGET_BYTES = 32 * 1024 * 1024
assert PAIR_W == LANES


def _attn_kernel(q_ref, zb_ref, k_ref, vt_ref, t_ref, o_ref, *, n_rows):
    step = pl.program_id(1)
    seq_len = n_rows * GRID_W
    mid = (WIN_ROWS - 2) * GRID_W
    lane_head = lax.broadcasted_iota(jnp.int32, (GRID_W, GROUP_W), 1) // HEAD_DIM
    low_half = lax.broadcasted_iota(jnp.int32, (GROUP_W, PAIR_W), 1) < GRID_W

    def row_body(j, carry):
        r = step * ROWS_PER_STEP + j
        rs = jnp.clip(r - WIN_ROWS // 2, 0, n_rows - WIN_ROWS)
        odd = rs % 2
        a = rs - odd
        x0 = jnp.where(odd == 1, rs + WIN_ROWS - 1, rs)
        x1 = jnp.where(odd == 1, rs, rs + 1)
        k_x0 = pl.multiple_of(x0 * GRID_W, GRID_W)
        k_x1 = pl.multiple_of(x1 * GRID_W, GRID_W)
        k_mid = pl.multiple_of((a + 2) * GRID_W, PAIR_W)
        t_x0 = pl.multiple_of((x0 - r + WIN_ROWS - 1) * GRID_W, GRID_W)
        t_x1 = pl.multiple_of((x1 - r + WIN_ROWS - 1) * GRID_W, GRID_W)
        t_mid = pl.multiple_of((a + 2 - r + WIN_ROWS - 1) * GRID_W, GRID_W)
        v_lo = pl.multiple_of(a * GRID_W, PAIR_W)
        v_hi = pl.multiple_of(jnp.minimum((a + WIN_ROWS) * GRID_W, seq_len - PAIR_W), PAIR_W)
        take_hi = jnp.logical_and(low_half, odd == 1)
        q0 = pl.multiple_of(j * GRID_W, GRID_W)
        for g in range(N_GROUPS):
            cols = slice(g * GROUP_W, (g + 1) * GROUP_W)
            qg = q_ref[0, pl.ds(q0, GRID_W), cols]
            zero = jnp.zeros_like(qg)
            qbd = jnp.concatenate(
                [jnp.where(lane_head == h, qg, zero) for h in range(HEADS_PER_GROUP)], axis=0)
            kw = jnp.concatenate([k_ref[0, pl.ds(k_x0, GRID_W), cols],
                                  k_ref[0, pl.ds(k_x1, GRID_W), cols],
                                  k_ref[0, pl.ds(k_mid, mid), cols]], axis=0)
            tb = jnp.concatenate([t_ref[pl.ds(t_x0, GRID_W), cols],
                                  t_ref[pl.ds(t_x1, GRID_W), cols],
                                  t_ref[pl.ds(t_mid, mid), cols]], axis=0)
            s = lax.dot_general(kw, qbd, (((1,), (1,)), ((), ())),
                                preferred_element_type=f32) + tb
            m = jnp.max(s, axis=0, keepdims=True)
            e = jnp.exp2(s - m)
            rinv = 1.0 / jnp.sum(e, axis=0, keepdims=True)
            v_first = jnp.where(take_hi, vt_ref[0, cols, pl.ds(v_hi, PAIR_W)],
                                vt_ref[0, cols, pl.ds(v_lo, PAIR_W)])
            vtw = jnp.concatenate([v_first, vt_ref[0, cols, pl.ds(k_mid, mid)]], axis=1)
            o2t = jnp.dot(vtw, e.astype(bf16), preferred_element_type=f32)
            z = jnp.concatenate([o2t[0:LANES, 0:LANES] * rinv[:, 0:LANES],
                                 o2t[LANES:, LANES:] * rinv[:, LANES:]], axis=0)
            zt = z.T
            og = jnp.where(lane_head % 2 == 0, zt[0:GRID_W, :], zt[GRID_W:, :])
            zb = zb_ref[0, pl.ds(q0, GRID_W), cols].astype(f32)
            o_ref[0, pl.ds(q0, GRID_W), cols] = (og * _silu(zb)).astype(bf16)
        return carry

    lax.fori_loop(0, ROWS_PER_STEP, row_body, 0, unroll=ATTN_ROW_UNROLL)


def _attn_branch(proj3d, vt, bias_table):
    b, l, _ = proj3d.shape
    n_rows = l // GRID_W
    tq = ROWS_PER_STEP * GRID_W

    def tile(col):
        return pl.BlockSpec((1, tq, D_MODEL), lambda bi, i: (bi, i, col))

    kv_bytes = 2 * l * D_MODEL * jnp.dtype(bf16).itemsize
    kv_mode = pl.Buffered(2 if 2 * kv_bytes <= KV_RESIDENT_BUDGET_BYTES else 1)

    return pl.pallas_call(
        functools.partial(_attn_kernel, n_rows=n_rows),
        grid=(b, n_rows // ROWS_PER_STEP),
        in_specs=[tile(COL_Q), tile(COL_Z_B),
                  pl.BlockSpec((1, l, D_MODEL), lambda bi, i: (bi, 0, COL_K), pipeline_mode=kv_mode),
                  pl.BlockSpec((1, D_MODEL, l), lambda bi, i: (bi, 0, 0), pipeline_mode=kv_mode),
                  _const_spec(bias_table.shape)],
        out_specs=pl.BlockSpec((1, tq, D_MODEL), lambda bi, i: (bi, i, 0)),
        out_shape=jax.ShapeDtypeStruct((b, l, D_MODEL), bf16),
        compiler_params=pltpu.CompilerParams(
            dimension_semantics=("arbitrary", "arbitrary"), vmem_limit_bytes=VMEM_LIMIT_BYTES),
        name="attn_branch",
    )(proj3d, proj3d, proj3d, vt, bias_table)


def _merge_kernel(x_ref, sa_ref, sb_ref, ga_ref, gb_ref, bg_ref, wa_ref, wb_ref, wo_ref, fg_ref, o_ref):
    y_a = jnp.dot(sa_ref[...], wa_ref[...], preferred_element_type=f32)
    y_b = jnp.dot(sb_ref[...], wb_ref[...], preferred_element_type=f32)
    g_a = _sigmoid(ga_ref[...].astype(f32) + bg_ref[:, :D_MODEL])
    g_b = _sigmoid(gb_ref[...].astype(f32) + bg_ref[:, D_MODEL:])
    merged = (g_a * y_a + g_b * y_b).astype(bf16)
    out = x_ref[...] + jnp.dot(merged, wo_ref[...], preferred_element_type=f32)
    ms = jnp.mean(out * out, axis=-1, keepdims=True)
    o_ref[...] = out * lax.rsqrt(ms + RMS_EPS) * fg_ref[...]


def _merge(x2d, s_a, s_b, proj2d, b_gate, w_pw_a, w_o_b, w_out, final_g):
    t = x2d.shape[0]
    tm = TM_MERGE

    def tile(col=0):
        return pl.BlockSpec((tm, D_MODEL), lambda i: (i, col))

    return pl.pallas_call(
        _merge_kernel,
        grid=(t // tm,),
        in_specs=[tile(), tile(), tile(), tile(COL_G_A), tile(COL_G_B),
                  _const_spec((1, 2 * D_MODEL)),
                  _const_spec((D_MODEL, D_MODEL)), _const_spec((D_MODEL, D_MODEL)),
                  _const_spec((D_MODEL, D_MODEL)), _const_spec((1, D_MODEL))],
        out_specs=tile(),
        out_shape=jax.ShapeDtypeStruct((t, D_MODEL), f32),
        compiler_params=pltpu.CompilerParams(
            dimension_semantics=("arbitrary",), vmem_limit_bytes=VMEM_LIMIT_BYTES),
        name="merge_out",
    )(x2d, s_a, s_b, proj2d, proj2d, b_gate, w_pw_a, w_o_b, w_out, final_g)


def _trunk(x, norm_g, w_in, b_gate, dw_w, dw_b, ln_g, ln_b, w_pw_a, rpb, w_o_b, w_out, final_g):
    assert norm_g.shape[0] == 1, "one layer"
    b, l, d = x.shape
    assert d == D_MODEL and l % (ROWS_PER_STEP * GRID_W) == 0 and l % TS_CONV == 0 and l % TM_PROJ == 0
    assert (b * l) % TM_MERGE == 0 and l // GRID_W >= WIN_ROWS + 2
    row = lambda v: v.reshape(1, -1)
    x2d = x.reshape(b * l, d)
    w = w_in[0].astype(bf16)
    v_cols = slice(W_IN_V_CHUNK * D_MODEL, (W_IN_V_CHUNK + 1) * D_MODEL)
    proj2d, vt = _in_proj(x2d, row(norm_g[0]), w, w[:, v_cols].T, b)
    proj3d = proj2d.reshape(b, l, -1)
    s_a = _conv_branch(proj3d, dw_w[0], row(dw_b[0]), row(ln_g[0]), row(ln_b[0]))
    s_b = _attn_branch(proj3d, vt, _bias_table(rpb[0]))
    y = _merge(x2d, s_a.reshape(b * l, d), s_b.reshape(b * l, d), proj2d, row(b_gate[0]),
               w_pw_a[0].astype(bf16), w_o_b[0].astype(bf16), w_out[0].astype(bf16), row(final_g))
    return y.reshape(b, l, d)


def kernel(x_prompt, x_sample, norm_g, w_in, b_gate, dw_w, dw_b, ln_g, ln_b, w_pw_a, rpb, w_o_b, w_out, final_g):
    params = (norm_g, w_in, b_gate, dw_w, dw_b, ln_g, ln_b, w_pw_a, rpb, w_o_b, w_out, final_g)
    return (_trunk(x_prompt, *params), _trunk(x_sample, *params))
```

```python
import functools
import math

import numpy as np
import jax
import jax.numpy as jnp
from jax import lax
from jax.experimental import pallas as pl
from jax.experimental.pallas import tpu as pltpu

D_MODEL = 1024
GRID_W = 64
CONV_K = 31
CONV_HALO = 16
SUBLANES = 8
LANES = 128
N_HEADS = 16
HEAD_DIM = 64
WIN_ROWS = 8
WIN_COLS = 16
HEADS_PER_GROUP = 4
GROUP_W = HEADS_PER_GROUP * HEAD_DIM
N_GROUPS = N_HEADS // HEADS_PER_GROUP
N_PROJ_CHUNKS = 7
COL_U, COL_Z_A, COL_Q, COL_K, COL_Z_B, COL_G_A, COL_G_B = range(N_PROJ_CHUNKS)
W_A_VAL, W_A_GLU, W_Z_A, W_Q, W_K, W_V, W_Z_B, W_G_A, W_G_B = range(9)
PROJ_SRC = {COL_Z_A: W_Z_A, COL_Q: W_Q, COL_K: W_K, COL_Z_B: W_Z_B, COL_G_A: W_G_A, COL_G_B: W_G_B}
RMS_EPS = 1e-6
LN_EPS = 1e-5
NEG_INF = -1e30
LOG2E = math.log2(math.e)
Q_SCALE = HEAD_DIM ** -0.5 * LOG2E

VMEM_LIMIT_BYTES = 56 * 1024 * 1024

TM_PROJ = 512
TS_CONV = 512
ROWS_PER_STEP = 8
TM_MERGE = 512

f32 = jnp.float32
bf16 = jnp.bfloat16


def _sigmoid(x):
    return jax.nn.sigmoid(x)


def _silu(x):
    return x * _sigmoid(x)


def _const_spec(shape):
    return pl.BlockSpec(shape, lambda *_: (0,) * len(shape), pipeline_mode=pl.Buffered(1))


def _inproj_kernel(x_ref, g_ref, w_ref, wvt_ref, o_ref, vt_ref):
    x = x_ref[...]
    ms = jnp.mean(x * x, axis=-1, keepdims=True)
    h = (x * lax.rsqrt(ms + RMS_EPS) * g_ref[...]).astype(bf16)
    def proj(src):
        return jnp.dot(h, w_ref[:, src * D_MODEL:(src + 1) * D_MODEL], preferred_element_type=f32)

    o_ref[:, COL_U * D_MODEL:(COL_U + 1) * D_MODEL] = (proj(W_A_VAL) * _sigmoid(proj(W_A_GLU))).astype(bf16)
    for col, src in PROJ_SRC.items():
        acc = proj(src)
        if col == COL_Q:
            acc = acc * Q_SCALE
        o_ref[:, col * D_MODEL:(col + 1) * D_MODEL] = acc.astype(bf16)
    vt = lax.dot_general(wvt_ref[...], h, (((1,), (1,)), ((), ())), preferred_element_type=f32)
    vt_ref[0] = vt.astype(bf16)


def _in_proj(x2d, norm_g, w_all, w_vt, batch):
    t = x2d.shape[0]
    d_main = N_PROJ_CHUNKS * D_MODEL
    tiles_per_batch = t // batch // TM_PROJ
    return pl.pallas_call(
        _inproj_kernel,
        grid=(t // TM_PROJ,),
        in_specs=[
            pl.BlockSpec((TM_PROJ, D_MODEL), lambda i: (i, 0)),
            _const_spec((1, D_MODEL)),
            _const_spec(w_all.shape),
            _const_spec((D_MODEL, D_MODEL)),
        ],
        out_specs=[
            pl.BlockSpec((TM_PROJ, d_main), lambda i: (i, 0)),
            pl.BlockSpec((1, D_MODEL, TM_PROJ), lambda i: (i // tiles_per_batch, 0, i % tiles_per_batch)),
        ],
        out_shape=[
            jax.ShapeDtypeStruct((t, d_main), bf16),
            jax.ShapeDtypeStruct((batch, D_MODEL, t // batch), bf16),
        ],
        compiler_params=pltpu.CompilerParams(
            dimension_semantics=("arbitrary",), vmem_limit_bytes=VMEM_LIMIT_BYTES),
        name="in_proj",
    )(x2d, norm_g, w_all, w_vt)


CONV_ROW_CHUNK = 64
CONV_LANE_CHUNK = LANES
LN_ROW_CHUNK = 16
LN_UNROLL = 8
CONV_BASE = CONV_HALO - CONV_K // 2
CONV_MAX_ALIGNED_OFF = (CONV_BASE + CONV_K - 1) // SUBLANES * SUBLANES


def _conv_kernel(uin_ref, za_ref, up_ref, un_ref,
                 dww_ref, dwb_ref, lng_ref, lnb_ref, o_ref, u_ref, c_ref):
    i = pl.program_id(1)
    n_tiles = pl.num_programs(1)
    ts = uin_ref.shape[1]

    u_ref[0, 0:CONV_HALO, :] = jnp.where(i > 0, up_ref[0].astype(f32), 0.0)
    u_ref[0, CONV_HALO:CONV_HALO + ts, :] = uin_ref[0].astype(f32)
    u_ref[0, CONV_HALO + ts:, :] = jnp.where(i < n_tiles - 1, un_ref[0].astype(f32), 0.0)

    n_shifted = ts + CONV_MAX_ALIGNED_OFF

    def conv_chunk(idx, carry):
        lanes = pl.ds(pl.multiple_of(idx * CONV_LANE_CHUNK, CONV_LANE_CHUNK), CONV_LANE_CHUNK)
        for s in range(1, SUBLANES):
            u_ref[s, 0:n_shifted, lanes] = u_ref[0, s:s + n_shifted, lanes]
        for r0 in range(0, ts, CONV_ROW_CHUNK):
            acc = jnp.broadcast_to(dwb_ref[:, lanes], (CONV_ROW_CHUNK, CONV_LANE_CHUNK))
            for s in range(SUBLANES):
                blk = u_ref[s, r0:r0 + CONV_ROW_CHUNK + CONV_MAX_ALIGNED_OFF, lanes]
                for k in range(CONV_K):
                    a, sk = divmod(CONV_BASE + k, SUBLANES)
                    if sk == s:
                        acc = acc + dww_ref[k:k + 1, lanes] * blk[a * SUBLANES:a * SUBLANES + CONV_ROW_CHUNK, :]
            c_ref[r0:r0 + CONV_ROW_CHUNK, lanes] = acc
        return carry

    lax.fori_loop(0, D_MODEL // CONV_LANE_CHUNK, conv_chunk, 0)

    def ln_chunk(idx, carry):
        r0 = pl.multiple_of(idx * LN_ROW_CHUNK, LN_ROW_CHUNK)
        rows = pl.ds(r0, LN_ROW_CHUNK)
        c = c_ref[rows, :]
        mu = jnp.mean(c, axis=-1, keepdims=True)
        d = c - mu
        var = jnp.mean(d * d, axis=-1, keepdims=True)
        y = d * lax.rsqrt(var + LN_EPS) * lng_ref[...] + lnb_ref[...]
        o_ref[0, rows, :] = (_silu(y) * _silu(za_ref[0, rows, :].astype(f32))).astype(bf16)
        return carry

    lax.fori_loop(0, ts // LN_ROW_CHUNK, ln_chunk, 0, unroll=LN_UNROLL)


def _conv_branch(proj3d, dw_w, dw_b, ln_g, ln_b):
    b, l, _ = proj3d.shape
    ts = TS_CONV
    halo_blocks_per_tile = ts // CONV_HALO
    n_halo_blocks = l // CONV_HALO

    def main(col):
        return pl.BlockSpec((1, ts, D_MODEL), lambda bi, i: (bi, i, col))

    def prev(col):
        return pl.BlockSpec((1, CONV_HALO, D_MODEL),
                            lambda bi, i: (bi, jnp.maximum(i * halo_blocks_per_tile - 1, 0), col))

    def nxt(col):
        return pl.BlockSpec((1, CONV_HALO, D_MODEL),
                            lambda bi, i: (bi, jnp.minimum((i + 1) * halo_blocks_per_tile, n_halo_blocks - 1), col))

    return pl.pallas_call(
        _conv_kernel,
        grid=(b, l // ts),
        in_specs=[
            main(COL_U), main(COL_Z_A), prev(COL_U), nxt(COL_U),
            _const_spec((CONV_K, D_MODEL)), _const_spec((1, D_MODEL)),
            _const_spec((1, D_MODEL)), _const_spec((1, D_MODEL)),
        ],
        out_specs=pl.BlockSpec((1, ts, D_MODEL), lambda bi, i: (bi, i, 0)),
        out_shape=jax.ShapeDtypeStruct((b, l, D_MODEL), bf16),
        scratch_shapes=[
            pltpu.VMEM((SUBLANES, ts + 2 * CONV_HALO, D_MODEL), f32),
            pltpu.VMEM((ts, D_MODEL), f32),
        ],
        compiler_params=pltpu.CompilerParams(
            dimension_semantics=("arbitrary", "arbitrary"), vmem_limit_bytes=VMEM_LIMIT_BYTES),
        name="conv_branch",
    )(proj3d, proj3d, proj3d, proj3d, dw_w, dw_b, ln_g, ln_b)


def _bias_table(rpb):
    qc = np.arange(GRID_W)[None, :]
    kc = np.arange(GRID_W)[:, None]
    win_start = np.clip(qc - WIN_COLS // 2, 0, GRID_W - WIN_COLS)
    valid = (kc >= win_start) & (kc < win_start + WIN_COLS)
    col_idx = np.clip(kc - qc, -(WIN_COLS - 1), WIN_COLS - 1) + WIN_COLS - 1
    onehot = jnp.asarray(col_idx[:, :, None] == np.arange(2 * WIN_COLS - 1)[None, None, :], f32)
    t = jnp.einsum('hdc,kqc->dkhq', rpb.astype(f32) * LOG2E, onehot, precision=lax.Precision.HIGHEST)
    t = jnp.where(jnp.asarray(valid)[None, :, None, :], t, NEG_INF)
    n_dr = 2 * WIN_ROWS - 1
    return t.reshape(n_dr * GRID_W, N_HEADS * GRID_W)


PAIR_W = 2 * GRID_W
ATTN_ROW_UNROLL = 8
KV_RESIDENT_BUDGET_BYTES = 32 * 1024 * 1024
assert PAIR_W == LANES


def _attn_kernel(q_ref, zb_ref, k_ref, vt_ref, t_ref, o_ref, *, n_rows):
    step = pl.program_id(1)
    seq_len = n_rows * GRID_W
    mid = (WIN_ROWS - 2) * GRID_W
    lane_head = lax.broadcasted_iota(jnp.int32, (GRID_W, GROUP_W), 1) // HEAD_DIM
    low_half = lax.broadcasted_iota(jnp.int32, (GROUP_W, PAIR_W), 1) < GRID_W

    def row_body(j, carry):
        r = step * ROWS_PER_STEP + j
        rs = jnp.clip(r - WIN_ROWS // 2, 0, n_rows - WIN_ROWS)
        odd = rs % 2
        a = rs - odd
        x0 = jnp.where(odd == 1, rs + WIN_ROWS - 1, rs)
        x1 = jnp.where(odd == 1, rs, rs + 1)
        k_x0 = pl.multiple_of(x0 * GRID_W, GRID_W)
        k_x1 = pl.multiple_of(x1 * GRID_W, GRID_W)
        k_mid = pl.multiple_of((a + 2) * GRID_W, PAIR_W)
        t_x0 = pl.multiple_of((x0 - r + WIN_ROWS - 1) * GRID_W, GRID_W)
        t_x1 = pl.multiple_of((x1 - r + WIN_ROWS - 1) * GRID_W, GRID_W)
        t_mid = pl.multiple_of((a + 2 - r + WIN_ROWS - 1) * GRID_W, GRID_W)
        v_lo = pl.multiple_of(a * GRID_W, PAIR_W)
        v_hi = pl.multiple_of(jnp.minimum((a + WIN_ROWS) * GRID_W, seq_len - PAIR_W), PAIR_W)
        take_hi = jnp.logical_and(low_half, odd == 1)
        q0 = pl.multiple_of(j * GRID_W, GRID_W)
        for g in range(N_GROUPS):
            cols = slice(g * GROUP_W, (g + 1) * GROUP_W)
            qg = q_ref[0, pl.ds(q0, GRID_W), cols]
            zero = jnp.zeros_like(qg)
            qbd = jnp.concatenate(
                [jnp.where(lane_head == h, qg, zero) for h in range(HEADS_PER_GROUP)], axis=0)
            kw = jnp.concatenate([k_ref[0, pl.ds(k_x0, GRID_W), cols],
                                  k_ref[0, pl.ds(k_x1, GRID_W), cols],
                                  k_ref[0, pl.ds(k_mid, mid), cols]], axis=0)
            tb = jnp.concatenate([t_ref[pl.ds(t_x0, GRID_W), cols],
                                  t_ref[pl.ds(t_x1, GRID_W), cols],
                                  t_ref[pl.ds(t_mid, mid), cols]], axis=0)
            s = lax.dot_general(kw, qbd, (((1,), (1,)), ((), ())),
                                preferred_element_type=f32) + tb
            m = jnp.max(s, axis=0, keepdims=True)
            e = jnp.exp2(s - m)
            rinv = 1.0 / jnp.sum(e, axis=0, keepdims=True)
            v_first = jnp.where(take_hi, vt_ref[0, cols, pl.ds(v_hi, PAIR_W)],
                                vt_ref[0, cols, pl.ds(v_lo, PAIR_W)])
            vtw = jnp.concatenate([v_first, vt_ref[0, cols, pl.ds(k_mid, mid)]], axis=1)
            o2t = jnp.dot(vtw, e.astype(bf16), preferred_element_type=f32)
            z = jnp.concatenate([o2t[0:LANES, 0:LANES] * rinv[:, 0:LANES],
                                 o2t[LANES:, LANES:] * rinv[:, LANES:]], axis=0)
            zt = z.T
            og = jnp.where(lane_head % 2 == 0, zt[0:GRID_W, :], zt[GRID_W:, :])
            zb = zb_ref[0, pl.ds(q0, GRID_W), cols].astype(f32)
            o_ref[0, pl.ds(q0, GRID_W), cols] = (og * _silu(zb)).astype(bf16)
        return carry

    lax.fori_loop(0, ROWS_PER_STEP, row_body, 0, unroll=ATTN_ROW_UNROLL)


def _attn_branch(proj3d, vt, bias_table):
    b, l, _ = proj3d.shape
    n_rows = l // GRID_W
    tq = ROWS_PER_STEP * GRID_W

    def tile(col):
        return pl.BlockSpec((1, tq, D_MODEL), lambda bi, i: (bi, i, col))

    kv_bytes = 2 * l * D_MODEL * jnp.dtype(bf16).itemsize
    kv_mode = pl.Buffered(2 if 2 * kv_bytes <= KV_RESIDENT_BUDGET_BYTES else 1)

    return pl.pallas_call(
        functools.partial(_attn_kernel, n_rows=n_rows),
        grid=(b, n_rows // ROWS_PER_STEP),
        in_specs=[tile(COL_Q), tile(COL_Z_B),
                  pl.BlockSpec((1, l, D_MODEL), lambda bi, i: (bi, 0, COL_K), pipeline_mode=kv_mode),
                  pl.BlockSpec((1, D_MODEL, l), lambda bi, i: (bi, 0, 0), pipeline_mode=kv_mode),
                  _const_spec(bias_table.shape)],
        out_specs=pl.BlockSpec((1, tq, D_MODEL), lambda bi, i: (bi, i, 0)),
        out_shape=jax.ShapeDtypeStruct((b, l, D_MODEL), bf16),
        compiler_params=pltpu.CompilerParams(
            dimension_semantics=("arbitrary", "arbitrary"), vmem_limit_bytes=VMEM_LIMIT_BYTES),
        name="attn_branch",
    )(proj3d, proj3d, proj3d, vt, bias_table)


def _merge_kernel(x_ref, sa_ref, sb_ref, ga_ref, gb_ref, bg_ref, wa_ref, wb_ref, wo_ref, fg_ref, o_ref):
    y_a = jnp.dot(sa_ref[...], wa_ref[...], preferred_element_type=f32)
    y_b = jnp.dot(sb_ref[...], wb_ref[...], preferred_element_type=f32)
    g_a = _sigmoid(ga_ref[...].astype(f32) + bg_ref[:, :D_MODEL])
    g_b = _sigmoid(gb_ref[...].astype(f32) + bg_ref[:, D_MODEL:])
    merged = (g_a * y_a + g_b * y_b).astype(bf16)
    out = x_ref[...] + jnp.dot(merged, wo_ref[...], preferred_element_type=f32)
    ms = jnp.mean(out * out, axis=-1, keepdims=True)
    o_ref[...] = out * lax.rsqrt(ms + RMS_EPS) * fg_ref[...]


def _merge(x2d, s_a, s_b, proj2d, b_gate, w_pw_a, w_o_b, w_out, final_g):
    t = x2d.shape[0]
    tm = TM_MERGE

    def tile(col=0):
        return pl.BlockSpec((tm, D_MODEL), lambda i: (i, col))

    return pl.pallas_call(
        _merge_kernel,
        grid=(t // tm,),
        in_specs=[tile(), tile(), tile(), tile(COL_G_A), tile(COL_G_B),
                  _const_spec((1, 2 * D_MODEL)),
                  _const_spec((D_MODEL, D_MODEL)), _const_spec((D_MODEL, D_MODEL)),
                  _const_spec((D_MODEL, D_MODEL)), _const_spec((1, D_MODEL))],
        out_specs=tile(),
        out_shape=jax.ShapeDtypeStruct((t, D_MODEL), f32),
        compiler_params=pltpu.CompilerParams(
            dimension_semantics=("arbitrary",), vmem_limit_bytes=VMEM_LIMIT_BYTES),
        name="merge_out",
    )(x2d, s_a, s_b, proj2d, proj2d, b_gate, w_pw_a, w_o_b, w_out, final_g)


def _trunk(x, norm_g, w_in, b_gate, dw_w, dw_b, ln_g, ln_b, w_pw_a, rpb, w_o_b, w_out, final_g):
    assert norm_g.shape[0] == 1, "one layer"
    b, l, d = x.shape
    assert d == D_MODEL and l % (ROWS_PER_STEP * GRID_W) == 0 and l % TS_CONV == 0 and l % TM_PROJ == 0
    assert (b * l) % TM_MERGE == 0 and l // GRID_W >= WIN_ROWS + 2
    row = lambda v: v.reshape(1, -1)
    x2d = x.reshape(b * l, d)
    w = w_in[0].astype(bf16)
    v_cols = slice(W_V * D_MODEL, (W_V + 1) * D_MODEL)
    proj2d, vt = _in_proj(x2d, row(norm_g[0]), w, w[:, v_cols].T, b)
    proj3d = proj2d.reshape(b, l, -1)
    s_a = _conv_branch(proj3d, dw_w[0], row(dw_b[0]), row(ln_g[0]), row(ln_b[0]))
    s_b = _attn_branch(proj3d, vt, _bias_table(rpb[0]))
    y = _merge(x2d, s_a.reshape(b * l, d), s_b.reshape(b * l, d), proj2d, row(b_gate[0]),
               w_pw_a[0].astype(bf16), w_o_b[0].astype(bf16), w_out[0].astype(bf16), row(final_g))
    return y.reshape(b, l, d)


def kernel(x_prompt, x_sample, norm_g, w_in, b_gate, dw_w, dw_b, ln_g, ln_b, w_pw_a, rpb, w_o_b, w_out, final_g):
    params = (norm_g, w_in, b_gate, dw_w, dw_b, ln_g, ln_b, w_pw_a, rpb, w_o_b, w_out, final_g)
    return (_trunk(x_prompt, *params), _trunk(x_sample, *params))
```

```python
import functools
import math

import numpy as np
import jax
import jax.numpy as jnp
from jax import lax
from jax.experimental import pallas as pl
from jax.experimental.pallas import tpu as pltpu

D_MODEL = 1024
GRID_W = 64
CONV_K = 31
CONV_HALO = 16
SUBLANES = 8
LANES = 128
N_HEADS = 16
HEAD_DIM = 64
WIN_ROWS = 8
WIN_COLS = 16
HEADS_PER_GROUP = 4
GROUP_W = HEADS_PER_GROUP * HEAD_DIM
N_GROUPS = N_HEADS // HEADS_PER_GROUP
N_PROJ_CHUNKS = 7
COL_U, COL_Z_A, COL_Q, COL_K, COL_Z_B, COL_G_A, COL_G_B = range(N_PROJ_CHUNKS)
W_A_VAL, W_A_GLU, W_Z_A, W_Q, W_K, W_V, W_Z_B, W_G_A, W_G_B = range(9)
PROJ_SRC = {COL_Z_A: W_Z_A, COL_Q: W_Q, COL_K: W_K, COL_Z_B: W_Z_B, COL_G_A: W_G_A, COL_G_B: W_G_B}
RMS_EPS = 1e-6
LN_EPS = 1e-5
NEG_INF = -1e30
LOG2E = math.log2(math.e)
Q_SCALE = HEAD_DIM ** -0.5 * LOG2E

VMEM_LIMIT_BYTES = 56 * 1024 * 1024

TM_PROJ = 512
TS_CONV = 512
ROWS_PER_STEP = 8
TM_MERGE = 512

f32 = jnp.float32
bf16 = jnp.bfloat16


def _sigmoid(x):
    return jax.nn.sigmoid(x)


def _silu(x):
    return x * _sigmoid(x)


def _const_spec(shape):
    return pl.BlockSpec(shape, lambda *_: (0,) * len(shape), pipeline_mode=pl.Buffered(1))


def _inproj_kernel(x_ref, g_ref, w_ref, wvt_ref, o_ref, vt_ref):
    x = x_ref[...]
    ms = jnp.mean(x * x, axis=-1, keepdims=True)
    h = (x * lax.rsqrt(ms + RMS_EPS) * g_ref[...]).astype(bf16)
    def proj(src):
        return jnp.dot(h, w_ref[:, src * D_MODEL:(src + 1) * D_MODEL], preferred_element_type=f32)

    o_ref[:, COL_U * D_MODEL:(COL_U + 1) * D_MODEL] = (proj(W_A_VAL) * _sigmoid(proj(W_A_GLU))).astype(bf16)
    for col, src in PROJ_SRC.items():
        acc = proj(src)
        if col == COL_Q:
            acc = acc * Q_SCALE
        elif col in (COL_Z_A, COL_Z_B):
            acc = _silu(acc)
        o_ref[:, col * D_MODEL:(col + 1) * D_MODEL] = acc.astype(bf16)
    vt = lax.dot_general(wvt_ref[...], h, (((1,), (1,)), ((), ())), preferred_element_type=f32)
    vt_ref[0] = vt.astype(bf16)


def _in_proj(x2d, norm_g, w_all, w_vt, batch):
    t = x2d.shape[0]
    d_main = N_PROJ_CHUNKS * D_MODEL
    tiles_per_batch = t // batch // TM_PROJ
    return pl.pallas_call(
        _inproj_kernel,
        grid=(t // TM_PROJ,),
        in_specs=[
            pl.BlockSpec((TM_PROJ, D_MODEL), lambda i: (i, 0)),
            _const_spec((1, D_MODEL)),
            _const_spec(w_all.shape),
            _const_spec((D_MODEL, D_MODEL)),
        ],
        out_specs=[
            pl.BlockSpec((TM_PROJ, d_main), lambda i: (i, 0)),
            pl.BlockSpec((1, D_MODEL, TM_PROJ), lambda i: (i // tiles_per_batch, 0, i % tiles_per_batch)),
        ],
        out_shape=[
            jax.ShapeDtypeStruct((t, d_main), bf16),
            jax.ShapeDtypeStruct((batch, D_MODEL, t // batch), bf16),
        ],
        compiler_params=pltpu.CompilerParams(
            dimension_semantics=("arbitrary",), vmem_limit_bytes=VMEM_LIMIT_BYTES),
        name="in_proj",
    )(x2d, norm_g, w_all, w_vt)


CONV_ROW_CHUNK = 64
CONV_LANE_CHUNK = LANES
LN_ROW_CHUNK = 16
LN_UNROLL = 16
CONV_BASE = CONV_HALO - CONV_K // 2
CONV_MAX_ALIGNED_OFF = (CONV_BASE + CONV_K - 1) // SUBLANES * SUBLANES


def _conv_kernel(uin_ref, za_ref, up_ref, un_ref,
                 dww_ref, dwb_ref, lng_ref, lnb_ref, o_ref, u_ref, c_ref):
    i = pl.program_id(1)
    n_tiles = pl.num_programs(1)
    ts = uin_ref.shape[1]

    u_ref[0, 0:CONV_HALO, :] = jnp.where(i > 0, up_ref[0].astype(f32), 0.0)
    u_ref[0, CONV_HALO:CONV_HALO + ts, :] = uin_ref[0].astype(f32)
    u_ref[0, CONV_HALO + ts:, :] = jnp.where(i < n_tiles - 1, un_ref[0].astype(f32), 0.0)

    n_shifted = ts + CONV_MAX_ALIGNED_OFF

    def conv_chunk(idx, carry):
        lanes = pl.ds(pl.multiple_of(idx * CONV_LANE_CHUNK, CONV_LANE_CHUNK), CONV_LANE_CHUNK)
        for s in range(1, SUBLANES):
            u_ref[s, 0:n_shifted, lanes] = u_ref[0, s:s + n_shifted, lanes]
        for r0 in range(0, ts, CONV_ROW_CHUNK):
            acc = jnp.broadcast_to(dwb_ref[:, lanes], (CONV_ROW_CHUNK, CONV_LANE_CHUNK))
            for s in range(SUBLANES):
                blk = u_ref[s, r0:r0 + CONV_ROW_CHUNK + CONV_MAX_ALIGNED_OFF, lanes]
                for k in range(CONV_K):
                    a, sk = divmod(CONV_BASE + k, SUBLANES)
                    if sk == s:
                        acc = acc + dww_ref[k:k + 1, lanes] * blk[a * SUBLANES:a * SUBLANES + CONV_ROW_CHUNK, :]
            c_ref[r0:r0 + CONV_ROW_CHUNK, lanes] = acc
        return carry

    lax.fori_loop(0, D_MODEL // CONV_LANE_CHUNK, conv_chunk, 0)

    def ln_chunk(idx, carry):
        r0 = pl.multiple_of(idx * LN_ROW_CHUNK, LN_ROW_CHUNK)
        rows = pl.ds(r0, LN_ROW_CHUNK)
        c = c_ref[rows, :]
        mu = jnp.mean(c, axis=-1, keepdims=True)
        d = c - mu
        var = jnp.mean(d * d, axis=-1, keepdims=True)
        y = d * lax.rsqrt(var + LN_EPS) * lng_ref[...] + lnb_ref[...]
        o_ref[0, rows, :] = (_silu(y) * za_ref[0, rows, :].astype(f32)).astype(bf16)
        return carry

    lax.fori_loop(0, ts // LN_ROW_CHUNK, ln_chunk, 0, unroll=LN_UNROLL)


def _conv_branch(proj3d, dw_w, dw_b, ln_g, ln_b):
    b, l, _ = proj3d.shape
    ts = TS_CONV
    halo_blocks_per_tile = ts // CONV_HALO
    n_halo_blocks = l // CONV_HALO

    def main(col):
        return pl.BlockSpec((1, ts, D_MODEL), lambda bi, i: (bi, i, col))

    def prev(col):
        return pl.BlockSpec((1, CONV_HALO, D_MODEL),
                            lambda bi, i: (bi, jnp.maximum(i * halo_blocks_per_tile - 1, 0), col))

    def nxt(col):
        return pl.BlockSpec((1, CONV_HALO, D_MODEL),
                            lambda bi, i: (bi, jnp.minimum((i + 1) * halo_blocks_per_tile, n_halo_blocks - 1), col))

    return pl.pallas_call(
        _conv_kernel,
        grid=(b, l // ts),
        in_specs=[
            main(COL_U), main(COL_Z_A), prev(COL_U), nxt(COL_U),
            _const_spec((CONV_K, D_MODEL)), _const_spec((1, D_MODEL)),
            _const_spec((1, D_MODEL)), _const_spec((1, D_MODEL)),
        ],
        out_specs=pl.BlockSpec((1, ts, D_MODEL), lambda bi, i: (bi, i, 0)),
        out_shape=jax.ShapeDtypeStruct((b, l, D_MODEL), bf16),
        scratch_shapes=[
            pltpu.VMEM((SUBLANES, ts + 2 * CONV_HALO, D_MODEL), f32),
            pltpu.VMEM((ts, D_MODEL), f32),
        ],
        compiler_params=pltpu.CompilerParams(
            dimension_semantics=("arbitrary", "arbitrary"), vmem_limit_bytes=VMEM_LIMIT_BYTES),
        name="conv_branch",
    )(proj3d, proj3d, proj3d, proj3d, dw_w, dw_b, ln_g, ln_b)


def _bias_table(rpb):
    qc = np.arange(GRID_W)[None, :]
    kc = np.arange(GRID_W)[:, None]
    win_start = np.clip(qc - WIN_COLS // 2, 0, GRID_W - WIN_COLS)
    valid = (kc >= win_start) & (kc < win_start + WIN_COLS)
    col_idx = np.clip(kc - qc, -(WIN_COLS - 1), WIN_COLS - 1) + WIN_COLS - 1
    onehot = jnp.asarray(col_idx[:, :, None] == np.arange(2 * WIN_COLS - 1)[None, None, :], f32)
    t = jnp.einsum('hdc,kqc->dkhq', rpb.astype(f32) * LOG2E, onehot, precision=lax.Precision.HIGHEST)
    t = jnp.where(jnp.asarray(valid)[None, :, None, :], t, NEG_INF)
    n_dr = 2 * WIN_ROWS - 1
    return t.reshape(n_dr * GRID_W, N_HEADS * GRID_W)


PAIR_W = 2 * GRID_W
ATTN_ROW_UNROLL = 8
KV_RESIDENT_BUDGET_BYTES = 32 * 1024 * 1024
assert PAIR_W == LANES


def _attn_kernel(q_ref, zb_ref, k_ref, vt_ref, t_ref, o_ref, *, n_rows):
    step = pl.program_id(1)
    seq_len = n_rows * GRID_W
    mid = (WIN_ROWS - 2) * GRID_W
    lane_head = lax.broadcasted_iota(jnp.int32, (GRID_W, GROUP_W), 1) // HEAD_DIM
    low_half = lax.broadcasted_iota(jnp.int32, (GROUP_W, PAIR_W), 1) < GRID_W

    def row_body(j, carry):
        r = step * ROWS_PER_STEP + j
        rs = jnp.clip(r - WIN_ROWS // 2, 0, n_rows - WIN_ROWS)
        odd = rs % 2
        a = rs - odd
        x0 = jnp.where(odd == 1, rs + WIN_ROWS - 1, rs)
        x1 = jnp.where(odd == 1, rs, rs + 1)
        k_x0 = pl.multiple_of(x0 * GRID_W, GRID_W)
        k_x1 = pl.multiple_of(x1 * GRID_W, GRID_W)
        k_mid = pl.multiple_of((a + 2) * GRID_W, PAIR_W)
        t_x0 = pl.multiple_of((x0 - r + WIN_ROWS - 1) * GRID_W, GRID_W)
        t_x1 = pl.multiple_of((x1 - r + WIN_ROWS - 1) * GRID_W, GRID_W)
        t_mid = pl.multiple_of((a + 2 - r + WIN_ROWS - 1) * GRID_W, GRID_W)
        v_lo = pl.multiple_of(a * GRID_W, PAIR_W)
        v_hi = pl.multiple_of(jnp.minimum((a + WIN_ROWS) * GRID_W, seq_len - PAIR_W), PAIR_W)
        take_hi = jnp.logical_and(low_half, odd == 1)
        q0 = pl.multiple_of(j * GRID_W, GRID_W)
        for g in range(N_GROUPS):
            cols = slice(g * GROUP_W, (g + 1) * GROUP_W)
            qg = q_ref[0, pl.ds(q0, GRID_W), cols]
            zero = jnp.zeros_like(qg)
            qbd = jnp.concatenate(
                [jnp.where(lane_head == h, qg, zero) for h in range(HEADS_PER_GROUP)], axis=0)
            kw = jnp.concatenate([k_ref[0, pl.ds(k_x0, GRID_W), cols],
                                  k_ref[0, pl.ds(k_x1, GRID_W), cols],
                                  k_ref[0, pl.ds(k_mid, mid), cols]], axis=0)
            tb = jnp.concatenate([t_ref[pl.ds(t_x0, GRID_W), cols],
                                  t_ref[pl.ds(t_x1, GRID_W), cols],
                                  t_ref[pl.ds(t_mid, mid), cols]], axis=0)
            s = lax.dot_general(kw, qbd, (((1,), (1,)), ((), ())),
                                preferred_element_type=f32) + tb
            m = jnp.max(s, axis=0, keepdims=True)
            e = jnp.exp2(s - m)
            rinv = 1.0 / jnp.sum(e, axis=0, keepdims=True)
            v_first = jnp.where(take_hi, vt_ref[0, cols, pl.ds(v_hi, PAIR_W)],
                                vt_ref[0, cols, pl.ds(v_lo, PAIR_W)])
            vtw = jnp.concatenate([v_first, vt_ref[0, cols, pl.ds(k_mid, mid)]], axis=1)
            o2t = jnp.dot(vtw, e.astype(bf16), preferred_element_type=f32)
            z = jnp.concatenate([o2t[0:LANES, 0:LANES] * rinv[:, 0:LANES],
                                 o2t[LANES:, LANES:] * rinv[:, LANES:]], axis=0)
            zt = z.T
            og = jnp.where(lane_head % 2 == 0, zt[0:GRID_W, :], zt[GRID_W:, :])
            zb = zb_ref[0, pl.ds(q0, GRID_W), cols].astype(f32)
            o_ref[0, pl.ds(q0, GRID_W), cols] = (og * zb).astype(bf16)
        return carry

    lax.fori_loop(0, ROWS_PER_STEP, row_body, 0, unroll=ATTN_ROW_UNROLL)


def _attn_branch(proj3d, vt, bias_table):
    b, l, _ = proj3d.shape
    n_rows = l // GRID_W
    tq = ROWS_PER_STEP * GRID_W

    def tile(col):
        return pl.BlockSpec((1, tq, D_MODEL), lambda bi, i: (bi, i, col))

    kv_bytes = 2 * l * D_MODEL * jnp.dtype(bf16).itemsize
    kv_mode = pl.Buffered(2 if 2 * kv_bytes <= KV_RESIDENT_BUDGET_BYTES else 1)

    return pl.pallas_call(
        functools.partial(_attn_kernel, n_rows=n_rows),
        grid=(b, n_rows // ROWS_PER_STEP),
        in_specs=[tile(COL_Q), tile(COL_Z_B),
                  pl.BlockSpec((1, l, D_MODEL), lambda bi, i: (bi, 0, COL_K), pipeline_mode=kv_mode),
                  pl.BlockSpec((1, D_MODEL, l), lambda bi, i: (bi, 0, 0), pipeline_mode=kv_mode),
                  _const_spec(bias_table.shape)],
        out_specs=pl.BlockSpec((1, tq, D_MODEL), lambda bi, i: (bi, i, 0)),
        out_shape=jax.ShapeDtypeStruct((b, l, D_MODEL), bf16),
        compiler_params=pltpu.CompilerParams(
            dimension_semantics=("arbitrary", "arbitrary"), vmem_limit_bytes=VMEM_LIMIT_BYTES),
        name="attn_branch",
    )(proj3d, proj3d, proj3d, vt, bias_table)


def _merge_kernel(x_ref, sa_ref, sb_ref, ga_ref, gb_ref, bg_ref, wa_ref, wb_ref, wo_ref, fg_ref, o_ref):
    y_a = jnp.dot(sa_ref[...], wa_ref[...], preferred_element_type=f32)
    y_b = jnp.dot(sb_ref[...], wb_ref[...], preferred_element_type=f32)
    g_a = _sigmoid(ga_ref[...].astype(f32) + bg_ref[:, :D_MODEL])
    g_b = _sigmoid(gb_ref[...].astype(f32) + bg_ref[:, D_MODEL:])
    merged = (g_a * y_a + g_b * y_b).astype(bf16)
    out = x_ref[...] + jnp.dot(merged, wo_ref[...], preferred_element_type=f32)
    ms = jnp.mean(out * out, axis=-1, keepdims=True)
    o_ref[...] = out * lax.rsqrt(ms + RMS_EPS) * fg_ref[...]


def _merge(x2d, s_a, s_b, proj2d, b_gate, w_pw_a, w_o_b, w_out, final_g):
    t = x2d.shape[0]
    tm = TM_MERGE

    def tile(col=0):
        return pl.BlockSpec((tm, D_MODEL), lambda i: (i, col))

    return pl.pallas_call(
        _merge_kernel,
        grid=(t // tm,),
        in_specs=[tile(), tile(), tile(), tile(COL_G_A), tile(COL_G_B),
                  _const_spec((1, 2 * D_MODEL)),
                  _const_spec((D_MODEL, D_MODEL)), _const_spec((D_MODEL, D_MODEL)),
                  _const_spec((D_MODEL, D_MODEL)), _const_spec((1, D_MODEL))],
        out_specs=tile(),
        out_shape=jax.ShapeDtypeStruct((t, D_MODEL), f32),
        compiler_params=pltpu.CompilerParams(
            dimension_semantics=("arbitrary",), vmem_limit_bytes=VMEM_LIMIT_BYTES),
        name="merge_out",
    )(x2d, s_a, s_b, proj2d, proj2d, b_gate, w_pw_a, w_o_b, w_out, final_g)


def _trunk(x, norm_g, w_in, b_gate, dw_w, dw_b, ln_g, ln_b, w_pw_a, rpb, w_o_b, w_out, final_g):
    assert norm_g.shape[0] == 1, "one layer"
    b, l, d = x.shape
    assert d == D_MODEL and l % (ROWS_PER_STEP * GRID_W) == 0 and l % TS_CONV == 0 and l % TM_PROJ == 0
    assert (b * l) % TM_MERGE == 0 and l // GRID_W >= WIN_ROWS + 2
    row = lambda v: v.reshape(1, -1)
    x2d = x.reshape(b * l, d)
    w = w_in[0].astype(bf16)
    v_cols = slice(W_V * D_MODEL, (W_V + 1) * D_MODEL)
    proj2d, vt = _in_proj(x2d, row(norm_g[0]), w, w[:, v_cols].T, b)
    proj3d = proj2d.reshape(b, l, -1)
    s_a = _conv_branch(proj3d, dw_w[0], row(dw_b[0]), row(ln_g[0]), row(ln_b[0]))
    s_b = _attn_branch(proj3d, vt, _bias_table(rpb[0]))
    y = _merge(x2d, s_a.reshape(b * l, d), s_b.reshape(b * l, d), proj2d, row(b_gate[0]),
               w_pw_a[0].astype(bf16), w_o_b[0].astype(bf16), w_out[0].astype(bf16), row(final_g))
    return y.reshape(b, l, d)


def kernel(x_prompt, x_sample, norm_g, w_in, b_gate, dw_w, dw_b, ln_g, ln_b, w_pw_a, rpb, w_o_b, w_out, final_g):
    params = (norm_g, w_in, b_gate, dw_w, dw_b, ln_g, ln_b, w_pw_a, rpb, w_o_b, w_out, final_g)
    return (_trunk(x_prompt, *params), _trunk(x_sample, *params))
```

```python
import functools
import math

import numpy as np
import jax
import jax.numpy as jnp
from jax import lax
from jax.experimental import pallas as pl
from jax.experimental.pallas import tpu as pltpu

D_MODEL = 1024
GRID_W = 64
CONV_K = 31
CONV_HALO = 16
SUBLANES = 8
LANES = 128
N_HEADS = 16
HEAD_DIM = 64
WIN_ROWS = 8
WIN_COLS = 16
HEADS_PER_GROUP = 4
GROUP_W = HEADS_PER_GROUP * HEAD_DIM
N_GROUPS = N_HEADS // HEADS_PER_GROUP
N_PROJ_CHUNKS = 7
COL_U, COL_Z_A, COL_Q, COL_K, COL_Z_B, COL_G_A, COL_G_B = range(N_PROJ_CHUNKS)
W_A_VAL, W_A_GLU, W_Z_A, W_Q, W_K, W_V, W_Z_B, W_G_A, W_G_B = range(9)
PROJ_SRC = {COL_Z_A: W_Z_A, COL_Q: W_Q, COL_K: W_K, COL_Z_B: W_Z_B, COL_G_A: W_G_A, COL_G_B: W_G_B}
RMS_EPS = 1e-6
LN_EPS = 1e-5
NEG_INF = -1e30
LOG2E = math.log2(math.e)
Q_SCALE = HEAD_DIM ** -0.5 * LOG2E

VMEM_LIMIT_BYTES = 56 * 1024 * 1024

TM_PROJ = 512
COL_TILE = 256
TS_CONV = 512
ROWS_PER_STEP = 8
TM_MERGE = 512

f32 = jnp.float32
bf16 = jnp.bfloat16


def _sigmoid(x):
    return jax.nn.sigmoid(x)


def _silu(x):
    return x * _sigmoid(x)


def _const_spec(shape):
    return pl.BlockSpec(shape, lambda *_: (0,) * len(shape), pipeline_mode=pl.Buffered(1))


def _inproj_kernel(x_ref, g_ref, w_ref, wvt_ref, o_ref, vt_ref):
    x = x_ref[...]
    ms = jnp.mean(x * x, axis=-1, keepdims=True)
    h = (x * lax.rsqrt(ms + RMS_EPS) * g_ref[...]).astype(bf16)
    for t in range(D_MODEL // COL_TILE):
        def proj(src):
            c0 = src * D_MODEL + t * COL_TILE
            return jnp.dot(h, w_ref[:, c0:c0 + COL_TILE], preferred_element_type=f32)

        def put(col, val):
            c0 = col * D_MODEL + t * COL_TILE
            o_ref[:, c0:c0 + COL_TILE] = val.astype(bf16)

        put(COL_U, proj(W_A_VAL) * _sigmoid(proj(W_A_GLU)))
        for col, src in PROJ_SRC.items():
            acc = proj(src)
            if col == COL_Q:
                acc = acc * Q_SCALE
            elif col in (COL_Z_A, COL_Z_B):
                acc = _silu(acc)
            put(col, acc)
    vt = lax.dot_general(wvt_ref[...], h, (((1,), (1,)), ((), ())), preferred_element_type=f32)
    vt_ref[0] = vt.astype(bf16)


def _in_proj(x2d, norm_g, w_all, w_vt, batch):
    t = x2d.shape[0]
    d_main = N_PROJ_CHUNKS * D_MODEL
    tiles_per_batch = t // batch // TM_PROJ
    return pl.pallas_call(
        _inproj_kernel,
        grid=(t // TM_PROJ,),
        in_specs=[
            pl.BlockSpec((TM_PROJ, D_MODEL), lambda i: (i, 0)),
            _const_spec((1, D_MODEL)),
            _const_spec(w_all.shape),
            _const_spec((D_MODEL, D_MODEL)),
        ],
        out_specs=[
            pl.BlockSpec((TM_PROJ, d_main), lambda i: (i, 0)),
            pl.BlockSpec((1, D_MODEL, TM_PROJ), lambda i: (i // tiles_per_batch, 0, i % tiles_per_batch)),
        ],
        out_shape=[
            jax.ShapeDtypeStruct((t, d_main), bf16),
            jax.ShapeDtypeStruct((batch, D_MODEL, t // batch), bf16),
        ],
        compiler_params=pltpu.CompilerParams(
            dimension_semantics=("arbitrary",), vmem_limit_bytes=VMEM_LIMIT_BYTES),
        name="in_proj",
    )(x2d, norm_g, w_all, w_vt)


CONV_ROW_CHUNK = 64
CONV_LANE_CHUNK = LANES
LN_ROW_CHUNK = 16
LN_UNROLL = 16
CONV_BASE = CONV_HALO - CONV_K // 2
CONV_MAX_ALIGNED_OFF = (CONV_BASE + CONV_K - 1) // SUBLANES * SUBLANES


def _conv_kernel(uin_ref, za_ref, up_ref, un_ref,
                 dww_ref, dwb_ref, lng_ref, lnb_ref, o_ref, u_ref, c_ref):
    i = pl.program_id(1)
    n_tiles = pl.num_programs(1)
    ts = uin_ref.shape[1]

    u_ref[0, 0:CONV_HALO, :] = jnp.where(i > 0, up_ref[0].astype(f32), 0.0)
    u_ref[0, CONV_HALO:CONV_HALO + ts, :] = uin_ref[0].astype(f32)
    u_ref[0, CONV_HALO + ts:, :] = jnp.where(i < n_tiles - 1, un_ref[0].astype(f32), 0.0)

    n_shifted = ts + CONV_MAX_ALIGNED_OFF

    def conv_chunk(idx, carry):
        lanes = pl.ds(pl.multiple_of(idx * CONV_LANE_CHUNK, CONV_LANE_CHUNK), CONV_LANE_CHUNK)
        for s in range(1, SUBLANES):
            u_ref[s, 0:n_shifted, lanes] = u_ref[0, s:s + n_shifted, lanes]
        for r0 in range(0, ts, CONV_ROW_CHUNK):
            acc = jnp.broadcast_to(dwb_ref[:, lanes], (CONV_ROW_CHUNK, CONV_LANE_CHUNK))
            for s in range(SUBLANES):
                blk = u_ref[s, r0:r0 + CONV_ROW_CHUNK + CONV_MAX_ALIGNED_OFF, lanes]
                for k in range(CONV_K):
                    a, sk = divmod(CONV_BASE + k, SUBLANES)
                    if sk == s:
                        acc = acc + dww_ref[k:k + 1, lanes] * blk[a * SUBLANES:a * SUBLANES + CONV_ROW_CHUNK, :]
            c_ref[r0:r0 + CONV_ROW_CHUNK, lanes] = acc
        return carry

    lax.fori_loop(0, D_MODEL // CONV_LANE_CHUNK, conv_chunk, 0)

    def ln_chunk(idx, carry):
        r0 = pl.multiple_of(idx * LN_ROW_CHUNK, LN_ROW_CHUNK)
        rows = pl.ds(r0, LN_ROW_CHUNK)
        c = c_ref[rows, :]
        mu = jnp.mean(c, axis=-1, keepdims=True)
        d = c - mu
        var = jnp.mean(d * d, axis=-1, keepdims=True)
        y = d * lax.rsqrt(var + LN_EPS) * lng_ref[...] + lnb_ref[...]
        o_ref[0, rows, :] = (_silu(y) * za_ref[0, rows, :].astype(f32)).astype(bf16)
        return carry

    lax.fori_loop(0, ts // LN_ROW_CHUNK, ln_chunk, 0, unroll=LN_UNROLL)


def _conv_branch(proj3d, dw_w, dw_b, ln_g, ln_b):
    b, l, _ = proj3d.shape
    ts = TS_CONV
    halo_blocks_per_tile = ts // CONV_HALO
    n_halo_blocks = l // CONV_HALO

    def main(col):
        return pl.BlockSpec((1, ts, D_MODEL), lambda bi, i: (bi, i, col))

    def prev(col):
        return pl.BlockSpec((1, CONV_HALO, D_MODEL),
                            lambda bi, i: (bi, jnp.maximum(i * halo_blocks_per_tile - 1, 0), col))

    def nxt(col):
        return pl.BlockSpec((1, CONV_HALO, D_MODEL),
                            lambda bi, i: (bi, jnp.minimum((i + 1) * halo_blocks_per_tile, n_halo_blocks - 1), col))

    return pl.pallas_call(
        _conv_kernel,
        grid=(b, l // ts),
        in_specs=[
            main(COL_U), main(COL_Z_A), prev(COL_U), nxt(COL_U),
            _const_spec((CONV_K, D_MODEL)), _const_spec((1, D_MODEL)),
            _const_spec((1, D_MODEL)), _const_spec((1, D_MODEL)),
        ],
        out_specs=pl.BlockSpec((1, ts, D_MODEL), lambda bi, i: (bi, i, 0)),
        out_shape=jax.ShapeDtypeStruct((b, l, D_MODEL), bf16),
        scratch_shapes=[
            pltpu.VMEM((SUBLANES, ts + 2 * CONV_HALO, D_MODEL), f32),
            pltpu.VMEM((ts, D_MODEL), f32),
        ],
        compiler_params=pltpu.CompilerParams(
            dimension_semantics=("arbitrary", "arbitrary"), vmem_limit_bytes=VMEM_LIMIT_BYTES),
        name="conv_branch",
    )(proj3d, proj3d, proj3d, proj3d, dw_w, dw_b, ln_g, ln_b)


def _bias_table(rpb):
    qc = np.arange(GRID_W)[None, :]
    kc = np.arange(GRID_W)[:, None]
    win_start = np.clip(qc - WIN_COLS // 2, 0, GRID_W - WIN_COLS)
    valid = (kc >= win_start) & (kc < win_start + WIN_COLS)
    col_idx = np.clip(kc - qc, -(WIN_COLS - 1), WIN_COLS - 1) + WIN_COLS - 1
    onehot = jnp.asarray(col_idx[:, :, None] == np.arange(2 * WIN_COLS - 1)[None, None, :], f32)
    t = jnp.einsum('hdc,kqc->dkhq', rpb.astype(f32) * LOG2E, onehot, precision=lax.Precision.HIGHEST)
    t = jnp.where(jnp.asarray(valid)[None, :, None, :], t, NEG_INF)
    n_dr = 2 * WIN_ROWS - 1
    return t.reshape(n_dr * GRID_W, N_HEADS * GRID_W)


PAIR_W = 2 * GRID_W
ATTN_ROW_UNROLL = 8
KV_RESIDENT_BUDGET_BYTES = 32 * 1024 * 1024
assert PAIR_W == LANES


def _attn_kernel(q_ref, zb_ref, k_ref, vt_ref, t_ref, o_ref, *, n_rows):
    step = pl.program_id(1)
    seq_len = n_rows * GRID_W
    mid = (WIN_ROWS - 2) * GRID_W
    lane_head = lax.broadcasted_iota(jnp.int32, (GRID_W, GROUP_W), 1) // HEAD_DIM
    low_half = lax.broadcasted_iota(jnp.int32, (GROUP_W, PAIR_W), 1) < GRID_W

    def row_body(j, carry):
        r = step * ROWS_PER_STEP + j
        rs = jnp.clip(r - WIN_ROWS // 2, 0, n_rows - WIN_ROWS)
        odd = rs % 2
        a = rs - odd
        x0 = jnp.where(odd == 1, rs + WIN_ROWS - 1, rs)
        x1 = jnp.where(odd == 1, rs, rs + 1)
        k_x0 = pl.multiple_of(x0 * GRID_W, GRID_W)
        k_x1 = pl.multiple_of(x1 * GRID_W, GRID_W)
        k_mid = pl.multiple_of((a + 2) * GRID_W, PAIR_W)
        t_x0 = pl.multiple_of((x0 - r + WIN_ROWS - 1) * GRID_W, GRID_W)
        t_x1 = pl.multiple_of((x1 - r + WIN_ROWS - 1) * GRID_W, GRID_W)
        t_mid = pl.multiple_of((a + 2 - r + WIN_ROWS - 1) * GRID_W, GRID_W)
        v_lo = pl.multiple_of(a * GRID_W, PAIR_W)
        v_hi = pl.multiple_of(jnp.minimum((a + WIN_ROWS) * GRID_W, seq_len - PAIR_W), PAIR_W)
        take_hi = jnp.logical_and(low_half, odd == 1)
        q0 = pl.multiple_of(j * GRID_W, GRID_W)
        for g in range(N_GROUPS):
            cols = slice(g * GROUP_W, (g + 1) * GROUP_W)
            qg = q_ref[0, pl.ds(q0, GRID_W), cols]
            zero = jnp.zeros_like(qg)
            qbd = jnp.concatenate(
                [jnp.where(lane_head == h, qg, zero) for h in range(HEADS_PER_GROUP)], axis=0)
            kw = jnp.concatenate([k_ref[0, pl.ds(k_x0, GRID_W), cols],
                                  k_ref[0, pl.ds(k_x1, GRID_W), cols],
                                  k_ref[0, pl.ds(k_mid, mid), cols]], axis=0)
            tb = jnp.concatenate([t_ref[pl.ds(t_x0, GRID_W), cols],
                                  t_ref[pl.ds(t_x1, GRID_W), cols],
                                  t_ref[pl.ds(t_mid, mid), cols]], axis=0)
            s = lax.dot_general(kw, qbd, (((1,), (1,)), ((), ())),
                                preferred_element_type=f32) + tb
            m = jnp.max(s, axis=0, keepdims=True)
            e = jnp.exp2(s - m)
            rinv = 1.0 / jnp.sum(e, axis=0, keepdims=True)
            v_first = jnp.where(take_hi, vt_ref[0, cols, pl.ds(v_hi, PAIR_W)],
                                vt_ref[0, cols, pl.ds(v_lo, PAIR_W)])
            vtw = jnp.concatenate([v_first, vt_ref[0, cols, pl.ds(k_mid, mid)]], axis=1)
            o2t = jnp.dot(vtw, e.astype(bf16), preferred_element_type=f32)
            z = jnp.concatenate([o2t[0:LANES, 0:LANES] * rinv[:, 0:LANES],
                                 o2t[LANES:, LANES:] * rinv[:, LANES:]], axis=0)
            zt = z.T
            og = jnp.where(lane_head % 2 == 0, zt[0:GRID_W, :], zt[GRID_W:, :])
            zb = zb_ref[0, pl.ds(q0, GRID_W), cols].astype(f32)
            o_ref[0, pl.ds(q0, GRID_W), cols] = (og * zb).astype(bf16)
        return carry

    lax.fori_loop(0, ROWS_PER_STEP, row_body, 0, unroll=ATTN_ROW_UNROLL)


def _attn_branch(proj3d, vt, bias_table):
    b, l, _ = proj3d.shape
    n_rows = l // GRID_W
    tq = ROWS_PER_STEP * GRID_W

    def tile(col):
        return pl.BlockSpec((1, tq, D_MODEL), lambda bi, i: (bi, i, col))

    kv_bytes = 2 * l * D_MODEL * jnp.dtype(bf16).itemsize
    kv_mode = pl.Buffered(2 if 2 * kv_bytes <= KV_RESIDENT_BUDGET_BYTES else 1)

    return pl.pallas_call(
        functools.partial(_attn_kernel, n_rows=n_rows),
        grid=(b, n_rows // ROWS_PER_STEP),
        in_specs=[tile(COL_Q), tile(COL_Z_B),
                  pl.BlockSpec((1, l, D_MODEL), lambda bi, i: (bi, 0, COL_K), pipeline_mode=kv_mode),
                  pl.BlockSpec((1, D_MODEL, l), lambda bi, i: (bi, 0, 0), pipeline_mode=kv_mode),
                  _const_spec(bias_table.shape)],
        out_specs=pl.BlockSpec((1, tq, D_MODEL), lambda bi, i: (bi, i, 0)),
        out_shape=jax.ShapeDtypeStruct((b, l, D_MODEL), bf16),
        compiler_params=pltpu.CompilerParams(
            dimension_semantics=("arbitrary", "arbitrary"), vmem_limit_bytes=VMEM_LIMIT_BYTES),
        name="attn_branch",
    )(proj3d, proj3d, proj3d, vt, bias_table)


def _merge_kernel(x_ref, sa_ref, sb_ref, ga_ref, gb_ref, bg_ref, wa_ref, wb_ref, wo_ref, fg_ref, o_ref):
    y_a = jnp.dot(sa_ref[...], wa_ref[...], preferred_element_type=f32)
    y_b = jnp.dot(sb_ref[...], wb_ref[...], preferred_element_type=f32)
    g_a = _sigmoid(ga_ref[...].astype(f32) + bg_ref[:, :D_MODEL])
    g_b = _sigmoid(gb_ref[...].astype(f32) + bg_ref[:, D_MODEL:])
    merged = (g_a * y_a + g_b * y_b).astype(bf16)
    out = x_ref[...] + jnp.dot(merged, wo_ref[...], preferred_element_type=f32)
    ms = jnp.mean(out * out, axis=-1, keepdims=True)
    o_ref[...] = out * lax.rsqrt(ms + RMS_EPS) * fg_ref[...]


def _merge(x2d, s_a, s_b, proj2d, b_gate, w_pw_a, w_o_b, w_out, final_g):
    t = x2d.shape[0]
    tm = TM_MERGE

    def tile(col=0):
        return pl.BlockSpec((tm, D_MODEL), lambda i: (i, col))

    return pl.pallas_call(
        _merge_kernel,
        grid=(t // tm,),
        in_specs=[tile(), tile(), tile(), tile(COL_G_A), tile(COL_G_B),
                  _const_spec((1, 2 * D_MODEL)),
                  _const_spec((D_MODEL, D_MODEL)), _const_spec((D_MODEL, D_MODEL)),
                  _const_spec((D_MODEL, D_MODEL)), _const_spec((1, D_MODEL))],
        out_specs=tile(),
        out_shape=jax.ShapeDtypeStruct((t, D_MODEL), f32),
        compiler_params=pltpu.CompilerParams(
            dimension_semantics=("arbitrary",), vmem_limit_bytes=VMEM_LIMIT_BYTES),
        name="merge_out",
    )(x2d, s_a, s_b, proj2d, proj2d, b_gate, w_pw_a, w_o_b, w_out, final_g)


def _trunk(x, norm_g, w_in, b_gate, dw_w, dw_b, ln_g, ln_b, w_pw_a, rpb, w_o_b, w_out, final_g):
    assert norm_g.shape[0] == 1, "one layer"
    b, l, d = x.shape
    assert d == D_MODEL and l % (ROWS_PER_STEP * GRID_W) == 0 and l % TS_CONV == 0 and l % TM_PROJ == 0
    assert (b * l) % TM_MERGE == 0 and l // GRID_W >= WIN_ROWS + 2
    row = lambda v: v.reshape(1, -1)
    x2d = x.reshape(b * l, d)
    w = w_in[0].astype(bf16)
    v_cols = slice(W_V * D_MODEL, (W_V + 1) * D_MODEL)
    proj2d, vt = _in_proj(x2d, row(norm_g[0]), w, w[:, v_cols].T, b)
    proj3d = proj2d.reshape(b, l, -1)
    s_a = _conv_branch(proj3d, dw_w[0], row(dw_b[0]), row(ln_g[0]), row(ln_b[0]))
    s_b = _attn_branch(proj3d, vt, _bias_table(rpb[0]))
    y = _merge(x2d, s_a.reshape(b * l, d), s_b.reshape(b * l, d), proj2d, row(b_gate[0]),
               w_pw_a[0].astype(bf16), w_o_b[0].astype(bf16), w_out[0].astype(bf16), row(final_g))
    return y.reshape(b, l, d)


def kernel(x_prompt, x_sample, norm_g, w_in, b_gate, dw_w, dw_b, ln_g, ln_b, w_pw_a, rpb, w_o_b, w_out, final_g):
    params = (norm_g, w_in, b_gate, dw_w, dw_b, ln_g, ln_b, w_pw_a, rpb, w_o_b, w_out, final_g)
    return (_trunk(x_prompt, *params), _trunk(x_sample, *params))
```
